```python
import math
import jax, jax.numpy as jnp
from jax import lax
import numpy as np

D_MODEL = 1024
BATCH = 16
SEQ = 2048
DEPTH = 1

HEAD_DIM = 64
N_HEADS = D_MODEL // HEAD_DIM
DSA_HEADS = N_HEADS // 2
NSA_HEADS = N_HEADS - DSA_HEADS
DSA_WIDTH = DSA_HEADS * HEAD_DIM
NSA_WIDTH = NSA_HEADS * HEAD_DIM
ROPE_THETA = 500000.0
IDX_HEADS = 8
IDX_DIM = 32
IDX_TOPK = 256
Q_BLOCK = 128
NSA_KV_GROUPS = 2
NSA_GROUP_SIZE = NSA_HEADS // NSA_KV_GROUPS
KV_WIDTH = NSA_KV_GROUPS * HEAD_DIM
CMP_BLOCK = 32
CMP_STRIDE = 16
CMP_HIDDEN = 2 * HEAD_DIM
SLC_BLOCK = 64
SLC_COUNT = 8
WINDOW = 512
NSA_Q_BLOCK = 32
FORCE_SCORE = 1.0e4
FFN_HIDDEN = -(-8 * D_MODEL // (3 * 256)) * 256
ALPHA = (2.0 * DEPTH) ** 0.25
BETA = (8.0 * DEPTH) ** -0.25
LN_EPS = 1e-5
IN_SPLITS = (DSA_WIDTH, HEAD_DIM, HEAD_DIM, IDX_HEADS * IDX_DIM, IDX_HEADS, IDX_DIM,
             NSA_WIDTH, KV_WIDTH, KV_WIDTH, KV_WIDTH, KV_WIDTH, KV_WIDTH, KV_WIDTH, 3 * NSA_HEADS)
IN_IS_VALUE = (False, False, True, False, False, False,
               False, False, True, False, True, False, True, False)
IN_COLS = sum(IN_SPLITS)

kernel_name = "hymba_dsa_nsa_deepnorm_layer"


def split_columns(p):
    outs, start = [], 0
    for n in IN_SPLITS:
        outs.append(p[..., start:start + n])
        start += n
    return outs


def rope_partial(x, positions):
    rot = x.shape[-1] // 4
    half = rot // 2
    inv_freq = jnp.power(ROPE_THETA, -jnp.arange(half, dtype=jnp.float32) / half)
    ang = positions.astype(jnp.float32)[:, :, None, None] * inv_freq
    cos = jnp.cos(ang).astype(x.dtype)
    sin = jnp.sin(ang).astype(x.dtype)
    x1, x2 = x[..., :half], x[..., half:rot]
    return jnp.concatenate([x1 * cos - x2 * sin, x2 * cos + x1 * sin, x[..., rot:]], axis=-1)


def masked_softmax(s, mask):
    s = jnp.where(mask, s.astype(jnp.float32), -jnp.inf)
    m = jnp.max(s, axis=-1, keepdims=True)
    m = jnp.where(jnp.isfinite(m), m, 0.0)
    p = jnp.exp(s - m)
    return p / jnp.maximum(jnp.sum(p, axis=-1, keepdims=True), 1e-30)


def layer_norm(x, g, b):
    xf = x.astype(jnp.float32)
    mu = jnp.mean(xf, axis=-1, keepdims=True)
    var = jnp.mean(jnp.square(xf - mu), axis=-1, keepdims=True)
    return ((xf - mu) * lax.rsqrt(var + LN_EPS)).astype(x.dtype) * g + b


def group_rms(o):
    of = o.astype(jnp.float32)
    return (of * lax.rsqrt(jnp.mean(of * of, axis=-1, keepdims=True) + LN_EPS)).astype(o.dtype)


def dsa_mixer(q, k, v, iq, iw, ik):
    B, S, H, dh = q.shape
    topk = min(IDX_TOPK, S // 4)
    nblk = S // Q_BLOCK
    key_pos = jnp.arange(S)
    scale = HEAD_DIM ** -0.5
    idx_scale = (IDX_HEADS ** -0.5) * (IDX_DIM ** -0.5)

    def block(i):
        s0 = i * Q_BLOCK
        qb = lax.dynamic_slice_in_dim(q, s0, Q_BLOCK, axis=1)
        iqb = lax.dynamic_slice_in_dim(iq, s0, Q_BLOCK, axis=1)
        iwb = lax.dynamic_slice_in_dim(iw, s0, Q_BLOCK, axis=1)
        qpos = s0 + jnp.arange(Q_BLOCK)
        causal = key_pos[None, :] <= qpos[:, None]
        logits = jnp.einsum('bqhd,bsd->bqhs', iqb, ik)
        index = jnp.einsum('bqh,bqhs->bqs', iwb, jax.nn.relu(logits)) * idx_scale
        index = jnp.where(causal[None], index.astype(jnp.float32), -jnp.inf)
        _, sel = lax.top_k(index, topk)
        kg = jax.vmap(lambda kb, ib: kb[ib])(k, sel)
        vg = jax.vmap(lambda vb, ib: vb[ib])(v, sel)
        valid = (sel <= qpos[None, :, None])[:, :, None, :]
        s = jnp.einsum('bqhd,bqkd->bqhk', qb, kg) * scale
        p = masked_softmax(s, valid)
        return jnp.einsum('bqhk,bqkd->bqhd', p.astype(vg.dtype), vg)

    out = lax.map(block, jnp.arange(nblk))
    return out.transpose(1, 0, 2, 3, 4).reshape(B, S, H * dh)


def compress(kv, pos_emb, w1, w2):
    B, S, G, dh = kv.shape
    n_c = (S - CMP_BLOCK) // CMP_STRIDE + 1
    idx = jnp.arange(n_c)[:, None] * CMP_STRIDE + jnp.arange(CMP_BLOCK)[None, :]
    blocks = kv[:, idx] + pos_emb[None, None, :, None, :]
    flat = blocks.transpose(0, 1, 3, 2, 4).reshape(B, n_c, G, CMP_BLOCK * dh)
    return jax.nn.gelu(flat @ w1) @ w2


def nsa_mixer(q, k_cmp, v_cmp, k_slc, v_slc, k_win, v_win, gates, cmp_pos, cmp_w1, cmp_w2):
    B, S, H, dh = q.shape
    G, R = NSA_KV_GROUPS, NSA_GROUP_SIZE
    q = q.reshape(B, S, G, R, dh)
    scale = HEAD_DIM ** -0.5
    kc = compress(k_cmp, cmp_pos[0], cmp_w1[0], cmp_w2[0])
    vc = compress(v_cmp, cmp_pos[1], cmp_w1[1], cmp_w2[1])
    n_c = kc.shape[1]
    cmp_start = jnp.arange(n_c) * CMP_STRIDE
    cmp_end = cmp_start + CMP_BLOCK - 1
    n_s = S // SLC_BLOCK
    n_sel = min(SLC_COUNT, n_s)
    slc_start = jnp.arange(n_s) * SLC_BLOCK
    overlap = ((cmp_start[:, None] < slc_start[None, :] + SLC_BLOCK)
               & (cmp_end[:, None] >= slc_start[None, :])).astype(jnp.float32)
    ks_blocks = k_slc.reshape(B, n_s, SLC_BLOCK, G, dh).transpose(0, 3, 1, 2, 4)
    vs_blocks = v_slc.reshape(B, n_s, SLC_BLOCK, G, dh).transpose(0, 3, 1, 2, 4)
    pad = ((0, 0), (WINDOW, 0), (0, 0), (0, 0))
    kw_pad = jnp.pad(k_win, pad)
    vw_pad = jnp.pad(v_win, pad)
    nblk = S // NSA_Q_BLOCK
    QB = NSA_Q_BLOCK
    j = jnp.arange(n_s)

    def block(i):
        s0 = i * QB
        qb = lax.dynamic_slice_in_dim(q, s0, QB, axis=1)
        gb = lax.dynamic_slice_in_dim(gates, s0, QB, axis=1)
        qpos = s0 + jnp.arange(QB)
        s_c = jnp.einsum('bqgrd,bcgd->bgrqc', qb, kc) * scale
        p_c = masked_softmax(s_c, cmp_end[None, :] <= qpos[:, None])
        o_c = jnp.einsum('bgrqc,bcgd->bqgrd', p_c.astype(vc.dtype), vc)
        imp = jnp.einsum('bgrqc,cj->bgqj', p_c, overlap)
        cur = qpos // SLC_BLOCK
        forced = (j[None] == 0) | (j[None] == cur[:, None]) | (j[None] == cur[:, None] - 1)
        visible = slc_start[None, :] <= qpos[:, None]
        imp = jnp.where(forced[None, None], FORCE_SCORE, imp)
        imp = jnp.where(visible[None, None], imp, -1.0)
        _, sel = lax.top_k(imp, n_sel)
        gather = jax.vmap(jax.vmap(lambda kb, ib: kb[ib]))
        ksel = gather(ks_blocks, sel).reshape(B, G, QB, n_sel * SLC_BLOCK, dh)
        vsel = gather(vs_blocks, sel).reshape(B, G, QB, n_sel * SLC_BLOCK, dh)
        tok = (sel[..., None] * SLC_BLOCK + jnp.arange(SLC_BLOCK)).reshape(B, G, QB, n_sel * SLC_BLOCK)
        smask = (tok <= qpos[None, None, :, None])[:, :, None]
        s_s = jnp.einsum('bqgrd,bgqkd->bgrqk', qb, ksel) * scale
        p_s = masked_softmax(s_s, smask)
        o_s = jnp.einsum('bgrqk,bgqkd->bqgrd', p_s.astype(vsel.dtype), vsel)
        kw = lax.dynamic_slice_in_dim(kw_pad, s0, WINDOW + QB, axis=1)
        vw = lax.dynamic_slice_in_dim(vw_pad, s0, WINDOW + QB, axis=1)
        wpos = s0 - WINDOW + jnp.arange(WINDOW + QB)
        diff = qpos[:, None] - wpos[None, :]
        wmask = (diff >= 0) & (diff < WINDOW) & (wpos[None, :] >= 0)
        s_w = jnp.einsum('bqgrd,bkgd->bgrqk', qb, kw) * scale
        p_w = masked_softmax(s_w, wmask)
        o_w = jnp.einsum('bgrqk,bkgd->bqgrd', p_w.astype(vw.dtype), vw)
        return gb[..., 0:1] * o_c + gb[..., 1:2] * o_s + gb[..., 2:3] * o_w

    out = lax.map(block, jnp.arange(nblk))
    return out.transpose(1, 0, 2, 3, 4, 5).reshape(B, S, H * dh)


def hybrid_layer(x, positions, w_in, b_in, cmp_pos, cmp_w1, cmp_w2, mix_gain, w_out,
                 ln1_g, ln1_b, w_gate, w_up, w_down, ln2_g, ln2_b):
    B, S, _ = x.shape
    G = NSA_KV_GROUPS
    (dq, dk, dv, iq, iw, ik, nq, kcm, vcm, ksl, vsl, kwn, vwn, ng) = split_columns(x @ w_in + b_in)
    dq = rope_partial(dq.reshape(B, S, DSA_HEADS, HEAD_DIM), positions)
    dk = rope_partial(dk.reshape(B, S, 1, HEAD_DIM), positions)[:, :, 0]
    iq = rope_partial(iq.reshape(B, S, IDX_HEADS, IDX_DIM), positions)
    ik = rope_partial(ik.reshape(B, S, 1, IDX_DIM), positions)[:, :, 0]
    o_dsa = dsa_mixer(dq, dk, dv, iq, iw, ik)
    nq = rope_partial(nq.reshape(B, S, NSA_HEADS, HEAD_DIM), positions)
    heads = lambda t: t.reshape(B, S, G, HEAD_DIM)
    kcm = rope_partial(heads(kcm), positions)
    ksl = rope_partial(heads(ksl), positions)
    kwn = rope_partial(heads(kwn), positions)
    gates = jax.nn.sigmoid(ng).reshape(B, S, G, NSA_GROUP_SIZE, 3)
    o_nsa = nsa_mixer(nq, kcm, heads(vcm), ksl, heads(vsl), kwn, heads(vwn), gates,
                      cmp_pos, cmp_w1, cmp_w2)
    mixed = jnp.concatenate([group_rms(o_dsa), group_rms(o_nsa)], axis=-1) * mix_gain
    h = layer_norm(ALPHA * x + mixed @ w_out, ln1_g, ln1_b)
    ffn = (jax.nn.silu(h @ w_gate) * (h @ w_up)) @ w_down
    return layer_norm(ALPHA * h + ffn, ln2_g, ln2_b)


def setup_inputs(seed: int = 0) -> dict:
    key = jax.random.key(seed)
    ks = jax.random.split(key, 16)

    def nrm(k, shape, scale):
        return jax.random.normal(k, shape, jnp.float32) * scale

    x = nrm(ks[0], (BATCH, SEQ, D_MODEL), 1.0)
    offset = jax.random.randint(ks[1], (BATCH, 1), 0, 4096, dtype=jnp.int32)
    positions = offset + jnp.arange(SEQ, dtype=jnp.int32)[None, :]
    in_keys = jax.random.split(ks[2], len(IN_SPLITS))
    w_in = jnp.concatenate(
        [nrm(k, (DEPTH, D_MODEL, n), D_MODEL ** -0.5 * (BETA if is_v else 1.0))
         for k, n, is_v in zip(in_keys, IN_SPLITS, IN_IS_VALUE)], axis=-1)
    b_in = nrm(ks[3], (DEPTH, IN_COLS), 0.01)
    cmp_pos = nrm(ks[4], (DEPTH, 2, CMP_BLOCK, HEAD_DIM), 0.02)
    cmp_w1 = nrm(ks[5], (DEPTH, 2, CMP_BLOCK * HEAD_DIM, CMP_HIDDEN), (CMP_BLOCK * HEAD_DIM) ** -0.5)
    cmp_w2 = nrm(ks[6], (DEPTH, 2, CMP_HIDDEN, HEAD_DIM), CMP_HIDDEN ** -0.5)
    mix_gain = 1.0 + nrm(ks[7], (DEPTH, D_MODEL), 0.02)
    w_out = nrm(ks[8], (DEPTH, D_MODEL, D_MODEL), D_MODEL ** -0.5 * BETA)
    ln1_g = 1.0 + nrm(ks[9], (DEPTH, D_MODEL), 0.02)
    ln1_b = nrm(ks[10], (DEPTH, D_MODEL), 0.01)
    w_gate = nrm(ks[11], (DEPTH, D_MODEL, FFN_HIDDEN), D_MODEL ** -0.5)
    w_up = nrm(ks[12], (DEPTH, D_MODEL, FFN_HIDDEN), D_MODEL ** -0.5)
    w_down = nrm(ks[13], (DEPTH, FFN_HIDDEN, D_MODEL), FFN_HIDDEN ** -0.5 * BETA)
    ln2_g = 1.0 + nrm(ks[14], (DEPTH, D_MODEL), 0.02)
    ln2_b = nrm(ks[15], (DEPTH, D_MODEL), 0.01)
    return {"x": x, "positions": positions, "w_in": w_in, "b_in": b_in,
            "cmp_pos": cmp_pos, "cmp_w1": cmp_w1, "cmp_w2": cmp_w2, "mix_gain": mix_gain,
            "w_out": w_out, "ln1_g": ln1_g, "ln1_b": ln1_b, "w_gate": w_gate, "w_up": w_up,
            "w_down": w_down, "ln2_g": ln2_g, "ln2_b": ln2_b}


def reference(x, positions, w_in, b_in, cmp_pos, cmp_w1, cmp_w2, mix_gain, w_out,
              ln1_g, ln1_b, w_gate, w_up, w_down, ln2_g, ln2_b):
    for l in range(DEPTH):
        x = hybrid_layer(x, positions, w_in[l], b_in[l], cmp_pos[l], cmp_w1[l], cmp_w2[l],
                         mix_gain[l], w_out[l], ln1_g[l], ln1_b[l], w_gate[l], w_up[l],
                         w_down[l], ln2_g[l], ln2_b[l])
    return x
```

```python
import functools
import math

import jax
import jax.numpy as jnp
import numpy as np
from jax import lax
from jax.experimental import pallas as pl
from jax.experimental.pallas import tpu as pltpu

D_MODEL = 1024
HEAD_DIM = 64
DSA_HEADS = 8
NSA_HEADS = 8
DSA_WIDTH = DSA_HEADS * HEAD_DIM
NSA_WIDTH = NSA_HEADS * HEAD_DIM
ROPE_THETA = 500000.0
IDX_HEADS = 8
IDX_DIM = 32
IDX_TOPK = 256
NSA_KV_GROUPS = 2
NSA_GROUP_SIZE = NSA_HEADS // NSA_KV_GROUPS
KV_WIDTH = NSA_KV_GROUPS * HEAD_DIM
CMP_BLOCK = 32
CMP_STRIDE = 16
CMP_HIDDEN = 2 * HEAD_DIM
SLC_BLOCK = 64
SLC_COUNT = 8
WINDOW = 512
FORCE_SCORE = 1.0e4
FFN_HIDDEN = 2816
DEPTH = 1
ALPHA = (2.0 * DEPTH) ** 0.25
LN_EPS = 1e-5
ATTN_SCALE = HEAD_DIM ** -0.5
IDX_SCALE = (IDX_HEADS ** -0.5) * (IDX_DIM ** -0.5)

LANES = 128
MXU_DTYPE = jnp.bfloat16
TQ = 128
CH = 128
TM_PROJ = 512
FFN_CHUNK = 256
VMEM_LIMIT = 56 * 1024 * 1024

_O_DQ, _O_DK, _O_IQ, _O_IW, _O_IK, _O_NQ = 0, 512, 640, 896, 904, 936
_O_KC, _O_KS, _O_KW, _O_NG = 1448, 1704, 1960, 2216
IN_COLS = 2240

_F_IK, _F_IW, _F_NG = 0, 32, 40

_INT_MIN = -2 ** 31
_KEY_NEG_INF = int(np.array(-np.inf, np.float32).view(np.int32)) ^ 0x7FFFFFFF


def _cparams(n_axes):
    return pltpu.CompilerParams(dimension_semantics=("arbitrary",) * n_axes,
                                vmem_limit_bytes=VMEM_LIMIT)


def _dot(a, b):
    return jnp.dot(a, b, preferred_element_type=jnp.float32)


def _dot_nt(a, b):
    return lax.dot_general(a, b, (((1,), (1,)), ((), ())), preferred_element_type=jnp.float32)


def _layer_norm_rows(v, g, b):
    mu = jnp.mean(v, axis=-1, keepdims=True)
    d = v - mu
    var = jnp.mean(d * d, axis=-1, keepdims=True)
    return d * lax.rsqrt(var + LN_EPS) * g + b


_NC = 2304


def _in_proj_kernel(x_ref, pos_ref, fr_ref, w_ref, b_ref,
                    dq_ref, nq_ref, iq_ref, kcf_ref, vcf_ref, ks_ref, vs_ref, kw_ref, vw_ref, dkv_ref, f_ref):
    xb = x_ref[...].astype(MXU_DTYPE)
    tm = xb.shape[0]
    lane = lax.broadcasted_iota(jnp.int32, (tm, LANES), 1)
    ang = pos_ref[...].astype(jnp.float32) * fr_ref[...]
    cc, ss = jnp.cos(ang), jnp.sin(ang)
    cc_r, ss_r = pltpu.roll(cc, 64, 1), pltpu.roll(ss, 64, 1)
    lo = lane < 64
    c64, s64 = jnp.where(lo, cc, cc_r), jnp.where(lo, ss, ss_r)
    c32, s32 = jnp.where(lo, cc_r, cc), jnp.where(lo, ss_r, ss)
    x1_64 = (lane % 64) < 8
    x1_32 = (lane % 32) < 4

    def rope64(y, c=c64, s=s64):
        p = jnp.where(x1_64, -pltpu.roll(y, LANES - 8, 1), pltpu.roll(y, 8, 1))
        return y * c + p * s

    def rope32(y, c=c32, s=s32):
        p = jnp.where(x1_32, -pltpu.roll(y, LANES - 4, 1), pltpu.roll(y, 4, 1))
        return y * c + p * s

    def proj(col):
        return _dot(xb, w_ref[:, col:col + 256]) + b_ref[:, col:col + 256]

    for blk in range(2):
        y = proj(256 * blk)
        for h in range(2):
            dq_ref[:, 256 * blk + 128 * h:256 * blk + 128 * (h + 1)] = (
                rope64(y[:, 128 * h:128 * (h + 1)]) * ATTN_SCALE).astype(dq_ref.dtype)
    for blk in range(2):
        y = proj(512 + 256 * blk)
        for h in range(2):
            nq_ref[:, 256 * blk + 128 * h:256 * blk + 128 * (h + 1)] = (
                rope64(y[:, 128 * h:128 * (h + 1)]) * ATTN_SCALE).astype(nq_ref.dtype)
    y = proj(1024)
    for h in range(2):
        iq_ref[:, 128 * h:128 * (h + 1)] = rope32(y[:, 128 * h:128 * (h + 1)]).astype(iq_ref.dtype)
    y = proj(1280)
    kcf_ref[...] = rope64(y[:, 0:128])
    vcf_ref[...] = y[:, 128:256]
    y = proj(1536)
    ks_ref[...] = rope64(y[:, 0:128]).astype(ks_ref.dtype)
    vs_ref[...] = y[:, 128:256].astype(vs_ref.dtype)
    y = proj(1792)
    kw_ref[...] = rope64(y[:, 0:128]).astype(kw_ref.dtype)
    vw_ref[...] = y[:, 128:256].astype(vw_ref.dtype)
    y = proj(2048)
    dkv_ref[...] = rope64(y[:, 0:128], jnp.where(lo, c64, 1.0), jnp.where(lo, s64, 0.0)).astype(dkv_ref.dtype)
    f = y[:, 128:256]
    is_ik = lane < _F_IW
    f = rope32(f, jnp.where(is_ik, c32, 1.0), jnp.where(is_ik, s32, 0.0))
    f_ref[...] = jnp.where(lane >= _F_NG, jax.nn.sigmoid(f), f)


def _in_proj(x2, pos2, fr, w_p, b_p):
    n = x2.shape[0]
    tm = TM_PROJ
    row = lambda w: pl.BlockSpec((tm, w), lambda i: (i, 0))
    full = lambda a: pl.BlockSpec(a.shape, lambda i: (0, 0))
    bf, f32 = MXU_DTYPE, jnp.float32
    outs = [(512, bf), (512, bf), (256, bf), (128, f32), (128, f32), (128, bf), (128, bf), (128, bf), (128, bf), (128, bf), (128, f32)]
    return pl.pallas_call(
        _in_proj_kernel,
        out_shape=[jax.ShapeDtypeStruct((n, w), dt) for w, dt in outs],
        grid=(n // tm,),
        in_specs=[row(D_MODEL), row(1), full(fr), full(w_p), full(b_p)],
        out_specs=[row(w) for w, _ in outs],
        compiler_params=_cparams(1),
        name="in_proj",
    )(x2, pos2, fr, w_p, b_p)


def _compress_kernel(kc_ref, vc_ref, pos_ref, w1_ref, w2_ref, out_ref):
    s = kc_ref.shape[0]
    n_chunk = s // CMP_STRIDE
    if n_chunk < LANES:
        out_ref[...] = jnp.zeros(out_ref.shape, out_ref.dtype)
    for kv, src_ref in enumerate((kc_ref, vc_ref)):
        top = [jnp.zeros((n_chunk, CMP_HIDDEN), jnp.float32) for _ in range(NSA_KV_GROUPS)]
        bot = [jnp.zeros((n_chunk, CMP_HIDDEN), jnp.float32) for _ in range(NSA_KV_GROUPS)]
        for l in range(CMP_STRIDE):
            rows_all = src_ref[pl.ds(l, n_chunk, stride=CMP_STRIDE), :]
            w_top = w1_ref[kv, l * HEAD_DIM:(l + 1) * HEAD_DIM, :]
            w_bot = w1_ref[kv, (CMP_STRIDE + l) * HEAD_DIM:(CMP_STRIDE + l + 1) * HEAD_DIM, :]
            for g in range(NSA_KV_GROUPS):
                rows = rows_all[:, HEAD_DIM * g:HEAD_DIM * (g + 1)]
                a = (rows + pos_ref[kv, l:l + 1, :]).astype(MXU_DTYPE)
                bm = (rows + pos_ref[kv, CMP_STRIDE + l:CMP_STRIDE + l + 1, :]).astype(MXU_DTYPE)
                top[g] = top[g] + _dot(a, w_top)
                bot[g] = bot[g] + _dot(bm, w_bot)
        for g in range(NSA_KV_GROUPS):
            col = kv * KV_WIDTH + g * HEAD_DIM
            h = top[g] + pltpu.roll(bot[g], n_chunk - 1, 0)
            y = _dot(jax.nn.gelu(h).astype(MXU_DTYPE), w2_ref[kv])
            ridx = lax.broadcasted_iota(jnp.int32, y.shape, 0)
            y = jnp.where(ridx < n_chunk - 1, y, 0.0)
            out_ref[0:n_chunk, col:col + HEAD_DIM] = y.astype(out_ref.dtype)


def _compress(kc3, vc3, cmp_pos, w1, w2):
    b, s, _ = kc3.shape
    src = pl.BlockSpec((None, s, KV_WIDTH), lambda i: (i, 0, 0))
    return pl.pallas_call(
        _compress_kernel,
        out_shape=jax.ShapeDtypeStruct((b, LANES, 2 * KV_WIDTH), MXU_DTYPE),
        grid=(b,),
        in_specs=[src, src,
                  pl.BlockSpec(cmp_pos.shape, lambda i: (0, 0, 0)),
                  pl.BlockSpec(w1.shape, lambda i: (0, 0, 0)),
                  pl.BlockSpec(w2.shape, lambda i: (0, 0, 0))],
        out_specs=pl.BlockSpec((None, LANES, 2 * KV_WIDTH), lambda i: (i, 0, 0)),
        compiler_params=_cparams(1),
        name="nsa_compress",
    )(kc3, vc3, cmp_pos, w1, w2)


def _stack_heads(q_tile, h0, nh):
    return jnp.concatenate([q_tile[:, HEAD_DIM * (h0 + r):HEAD_DIM * (h0 + r + 1)] for r in range(nh)], axis=0)


def _masked_attention(q_st, nh, k_ref, v_ref, kcol, vcol, c_lo, c_hi, mask_fn, s_scr, m_scr, l_scr, acc_scr):
    rows = nh * TQ
    m_scr[0:rows, :] = jnp.full((rows, CH), -jnp.inf, jnp.float32)

    def score_body(c, carry):
        off = pl.multiple_of(c * CH, CH)
        kc = k_ref[pl.ds(off, CH), :][:, kcol:kcol + HEAD_DIM]
        s = _dot_nt(q_st, kc).reshape(nh, TQ, CH)
        s = jnp.where(mask_fn(c)[None], s, -jnp.inf).reshape(rows, CH)
        s_scr[c, 0:rows, :] = s
        m_scr[0:rows, :] = jnp.maximum(m_scr[0:rows, :], s)
        return carry

    lax.fori_loop(c_lo, c_hi, score_body, 0)
    m = jnp.max(m_scr[0:rows, :], axis=-1, keepdims=True)
    m = jnp.where(m > -jnp.inf, m, 0.0)
    l_scr[0:rows, :] = jnp.zeros((rows, CH), jnp.float32)
    acc_scr[0:rows, :] = jnp.zeros((rows, HEAD_DIM), jnp.float32)

    def pv_body(c, carry):
        off = pl.multiple_of(c * CH, CH)
        p = jnp.exp(s_scr[c, 0:rows, :] - m)
        l_scr[0:rows, :] = l_scr[0:rows, :] + p
        vc = v_ref[pl.ds(off, CH), :][:, vcol:vcol + HEAD_DIM]
        acc_scr[0:rows, :] = acc_scr[0:rows, :] + _dot(p.astype(MXU_DTYPE), vc)
        return carry

    lax.fori_loop(c_lo, c_hi, pv_body, 0)
    l = jnp.sum(l_scr[0:rows, :], axis=-1, keepdims=True)
    return acc_scr[0:rows, :] / jnp.maximum(l, 1e-30)


def _group_rms_store(o, gain_ref, o_ref):
    ms = jnp.mean(o * o, axis=-1, keepdims=True)
    o_ref[...] = (o * lax.rsqrt(ms + LN_EPS) * gain_ref[...]).astype(o_ref.dtype)


def _sortable_key(v):
    bits = lax.bitcast_convert_type(v + 0.0, jnp.int32)
    return jnp.where(bits < 0, bits ^ 0x7FFFFFFF, bits)


def _dsa_kernel(topk, dq_ref, iq_ref, fq_ref, dkv_ref, fk_ref, tri_ref, gain_ref, o_ref,
                key_scr, s_scr, m_scr, l_scr, acc_scr):
    i = pl.program_id(1)
    nk = i + 1
    row = lax.broadcasted_iota(jnp.int32, (TQ, CH), 0)
    lane = lax.broadcasted_iota(jnp.int32, (TQ, CH), 1)
    qpos = i * TQ + row

    iq = iq_ref[...]
    iq_h = [iq[:, IDX_DIM * h:IDX_DIM * (h + 1)] for h in range(IDX_HEADS)]
    iw = fq_ref[:, _F_IW:_F_IW + IDX_HEADS]

    def index_body(c, carry):
        off = pl.multiple_of(c * CH, CH)
        ikc = fk_ref[pl.ds(off, CH), _F_IK:_F_IK + IDX_DIM].astype(MXU_DTYPE)
        acc = jnp.zeros((TQ, CH), jnp.float32)
        for h in range(IDX_HEADS):
            acc = acc + iw[:, h:h + 1] * jnp.maximum(_dot_nt(iq_h[h], ikc), 0.0)
        idx = jnp.where(c * CH + lane <= qpos, acc * IDX_SCALE, -jnp.inf)
        key_scr[c] = _sortable_key(idx)
        return carry

    lax.fori_loop(0, nk, index_body, 0)

    def count_ge(cand):
        cb = jnp.broadcast_to(cand, (TQ, CH))

        def body(c, acc):
            return acc + jnp.where(key_scr[c] >= cb, 1.0, 0.0)

        part = lax.fori_loop(0, nk, body, jnp.zeros((TQ, CH), jnp.float32))
        return jnp.sum(part, axis=-1, keepdims=True)

    kf = jnp.float32(topk)
    n_all = (nk * CH).astype(jnp.float32)
    lo0 = jnp.full((TQ, 1), _INT_MIN, jnp.int32)
    cnt0 = jnp.zeros((TQ, 1), jnp.float32) + n_all
    cand = jnp.zeros((TQ, 1), jnp.int32)
    cnt = count_ge(cand)
    ok = cnt >= kf
    lo0, cnt0 = jnp.where(ok, cand, lo0), jnp.where(ok, cnt, cnt0)

    def bit_body(t, carry):
        lo, cnt_lo = carry
        cand = lo + jnp.left_shift(jnp.int32(1), 30 - t)
        cnt = count_ge(cand)
        ok = cnt >= kf
        return jnp.where(ok, cand, lo), jnp.where(ok, cnt, cnt_lo)

    tau, cnt_ge_tau = lax.fori_loop(0, 31, bit_body, (lo0, cnt0))

    tie_rows = (cnt_ge_tau > kf) & (tau > _KEY_NEG_INF)

    @pl.when(jnp.max(jnp.where(tie_rows, 1.0, 0.0)) > 0.0)
    def _():
        need = kf - count_ge(tau + 1)
        taub = jnp.broadcast_to(tau, (TQ, CH))
        live = jnp.broadcast_to(tie_rows, (TQ, CH))

        def tie_body(c, seen):
            k = key_scr[c]
            tie = (k == taub) & live
            tf = jnp.where(tie, 1.0, 0.0)
            before = seen + _dot(tf.astype(MXU_DTYPE), tri_ref[...])
            key_scr[c] = jnp.where(tie & (before >= need), taub - 1, k)
            return seen + jnp.sum(tf, axis=-1, keepdims=True)

        lax.fori_loop(0, nk, tie_body, jnp.zeros((TQ, 1), jnp.float32))

    taub = jnp.broadcast_to(tau, (TQ, CH))
    q_st = _stack_heads(dq_ref[...], 0, DSA_HEADS)

    def mask_fn(c):
        return (key_scr[c] >= taub) & (c * CH + lane <= qpos)

    o_st = _masked_attention(q_st, DSA_HEADS, dkv_ref, dkv_ref, 0, HEAD_DIM,
                             0, nk, mask_fn, s_scr, m_scr, l_scr, acc_scr)
    o = jnp.concatenate([o_st[TQ * h:TQ * (h + 1), :] for h in range(DSA_HEADS)], axis=1)
    _group_rms_store(o, gain_ref, o_ref)


def _dsa(dq3, iq3, f3, dkv3, tri, gain):
    b, s, _ = dq3.shape
    nk = s // CH
    topk = min(IDX_TOPK, s // 4)
    qblk = lambda w: pl.BlockSpec((None, TQ, w), lambda bi, i: (bi, i, 0))
    sblk = lambda w: pl.BlockSpec((None, s, w), lambda bi, i: (bi, 0, 0))
    const = lambda a: pl.BlockSpec(a.shape, lambda bi, i: (0, 0))
    rows = DSA_HEADS * TQ
    return pl.pallas_call(
        functools.partial(_dsa_kernel, topk),
        out_shape=jax.ShapeDtypeStruct((b, s, DSA_WIDTH), MXU_DTYPE),
        grid=(b, s // TQ),
        in_specs=[qblk(DSA_WIDTH), qblk(IDX_HEADS * IDX_DIM), qblk(LANES), sblk(2 * HEAD_DIM), sblk(LANES),
                  const(tri), const(gain)],
        out_specs=qblk(DSA_WIDTH),
        scratch_shapes=[pltpu.VMEM((nk, TQ, CH), jnp.int32),
                        pltpu.VMEM((nk, rows, CH), jnp.float32),
                        pltpu.VMEM((rows, CH), jnp.float32),
                        pltpu.VMEM((rows, CH), jnp.float32),
                        pltpu.VMEM((rows, HEAD_DIM), jnp.float32)],
        compiler_params=_cparams(2),
        name="dsa_attention",
    )(dq3, iq3, f3, dkv3, f3, tri, gain)


def _split3(v):
    a = v.astype(MXU_DTYPE)
    r = v - a.astype(jnp.float32)
    b = r.astype(MXU_DTYPE)
    c = (r - b.astype(jnp.float32)).astype(MXU_DTYPE)
    return a, b, c


def _nsa_kernel(n_cmp, n_slc, n_sel, nq_ref, f_ref, kcv_ref, ks_ref, vs_ref, kw_ref, vw_ref, ov_ref, ex_ref, gain_ref,
                o_ref, s_scr, m_scr, l_scr, acc_scr):
    i = pl.program_id(1)
    row = lax.broadcasted_iota(jnp.int32, (TQ, CH), 0)
    lane = lax.broadcasted_iota(jnp.int32, (TQ, CH), 1)
    qpos = i * TQ + row
    nq = nq_ref[...]
    gates = f_ref[:, _F_NG:_F_NG + 3 * NSA_HEADS]
    R = NSA_GROUP_SIZE
    rows = R * TQ
    heads_out = []
    for g in range(NSA_KV_GROUPS):
        q_st = _stack_heads(nq, R * g, R)
        kc = kcv_ref[:, HEAD_DIM * g:HEAD_DIM * (g + 1)]
        vc = kcv_ref[:, KV_WIDTH + HEAD_DIM * g:KV_WIDTH + HEAD_DIM * (g + 1)]
        cmask = (CMP_STRIDE * lane + CMP_BLOCK - 1 <= qpos) & (lane < n_cmp)
        s = _dot_nt(q_st, kc).reshape(R, TQ, CH)
        s = jnp.where(cmask[None], s, -jnp.inf)
        m = jnp.max(s, axis=-1, keepdims=True)
        m = jnp.where(m > -jnp.inf, m, 0.0)
        p = jnp.exp(s - m)
        p = p / jnp.maximum(jnp.sum(p, axis=-1, keepdims=True), 1e-30)
        o_c = _dot(p.reshape(rows, CH).astype(MXU_DTYPE), vc)
        psum = p[0]
        for r in range(1, R):
            psum = psum + p[r]
        pa, pb, pc = _split3(psum)
        ov = ov_ref[...]
        imp = _dot(pa, ov) + _dot(pb, ov) + _dot(pc, ov)
        cur = qpos // SLC_BLOCK
        forced = (lane == 0) | (lane == cur) | (lane == cur - 1)
        visible = (SLC_BLOCK * lane <= qpos) & (lane < n_slc)
        work = jnp.where(forced, FORCE_SCORE, imp)
        work = jnp.where(visible, work, -1.0)
        work = jnp.where(lane < n_slc, work, -2.0)
        sel = jnp.zeros((TQ, CH), jnp.bool_)
        for _ in range(n_sel):
            mx = jnp.max(work, axis=-1, keepdims=True)
            first = jnp.min(jnp.where(work == mx, lane, CH), axis=-1, keepdims=True)
            hit = lane == first
            sel = sel | hit
            work = jnp.where(hit, -3.0, work)
        selb = jnp.where(sel & visible, 1.0, 0.0).astype(MXU_DTYPE)

        def slc_mask(c):
            return (_dot(selb, ex_ref[c]) > 0.5) & (c * CH + lane <= qpos)

        o_s = _masked_attention(q_st, R, ks_ref, vs_ref, HEAD_DIM * g, HEAD_DIM * g, 0, i + 1, slc_mask,
                                s_scr, m_scr, l_scr, acc_scr)

        def win_mask(c):
            d = qpos - (c * CH + lane)
            return (d >= 0) & (d < WINDOW)

        o_w = _masked_attention(q_st, R, kw_ref, vw_ref, HEAD_DIM * g, HEAD_DIM * g, jnp.maximum(i - WINDOW // CH, 0), i + 1,
                                win_mask, s_scr, m_scr, l_scr, acc_scr)
        for r in range(R):
            gc = 3 * (R * g + r)
            sl = slice(TQ * r, TQ * (r + 1))
            heads_out.append(gates[:, gc:gc + 1] * o_c[sl] + gates[:, gc + 1:gc + 2] * o_s[sl]
                             + gates[:, gc + 2:gc + 3] * o_w[sl])
    _group_rms_store(jnp.concatenate(heads_out, axis=1), gain_ref, o_ref)


def _nsa(nq3, f3, kcv3, ks3, vs3, kw3, vw3, ov, ex, gain):
    b, s, _ = nq3.shape
    nk = s // CH
    n_slc = s // SLC_BLOCK
    n_sel = min(SLC_COUNT, n_slc)
    qblk = lambda w: pl.BlockSpec((None, TQ, w), lambda bi, i: (bi, i, 0))
    sblk = lambda n, w: pl.BlockSpec((None, n, w), lambda bi, i: (bi, 0, 0))
    rows = NSA_GROUP_SIZE * TQ
    return pl.pallas_call(
        functools.partial(_nsa_kernel, s // CMP_STRIDE - 1, n_slc, n_sel),
        out_shape=jax.ShapeDtypeStruct((b, s, NSA_WIDTH), MXU_DTYPE),
        grid=(b, s // TQ),
        in_specs=[qblk(NSA_WIDTH), qblk(LANES), sblk(kcv3.shape[1], 2 * KV_WIDTH),
                  sblk(s, KV_WIDTH), sblk(s, KV_WIDTH), sblk(s, KV_WIDTH), sblk(s, KV_WIDTH),
                  pl.BlockSpec(ov.shape, lambda bi, i: (0, 0)),
                  pl.BlockSpec(ex.shape, lambda bi, i: (0, 0, 0)),
                  pl.BlockSpec(gain.shape, lambda bi, i: (0, 0))],
        out_specs=qblk(NSA_WIDTH),
        scratch_shapes=[pltpu.VMEM((nk, rows, CH), jnp.float32),
                        pltpu.VMEM((rows, CH), jnp.float32),
                        pltpu.VMEM((rows, CH), jnp.float32),
                        pltpu.VMEM((rows, HEAD_DIM), jnp.float32)],
        compiler_params=_cparams(2),
        name="nsa_attention",
    )(nq3, f3, kcv3, ks3, vs3, kw3, vw3, ov, ex, gain)


def _out_proj_kernel(od_ref, on_ref, x_ref, w_ref, g_ref, b_ref, h_ref):
    y = _dot(od_ref[...], w_ref[0:DSA_WIDTH, :]) + _dot(on_ref[...], w_ref[DSA_WIDTH:, :])
    h_ref[...] = _layer_norm_rows(ALPHA * x_ref[...] + y, g_ref[...], b_ref[...])


def _out_proj(od, on, x2, w, g, b):
    n = x2.shape[0]
    tm = TM_PROJ
    row = lambda w_: pl.BlockSpec((tm, w_), lambda i: (i, 0))
    full = lambda a: pl.BlockSpec(a.shape, lambda i: (0, 0))
    return pl.pallas_call(
        _out_proj_kernel,
        out_shape=jax.ShapeDtypeStruct((n, D_MODEL), jnp.float32),
        grid=(n // tm,),
        in_specs=[row(DSA_WIDTH), row(NSA_WIDTH), row(D_MODEL), full(w), full(g), full(b)],
        out_specs=row(D_MODEL),
        compiler_params=_cparams(1),
        name="out_proj_ln",
    )(od, on, x2, w, g, b)


def _ffn_kernel(h_ref, wg_ref, wu_ref, wd_ref, g_ref, b_ref, o_ref):
    h = h_ref[...]
    hb = h.astype(MXU_DTYPE)
    acc = jnp.zeros(h.shape, jnp.float32)
    for c in range(FFN_HIDDEN // FFN_CHUNK):
        sl = slice(FFN_CHUNK * c, FFN_CHUNK * (c + 1))
        gate = _dot(hb, wg_ref[:, sl])
        up = _dot(hb, wu_ref[:, sl])
        acc = acc + _dot((jax.nn.silu(gate) * up).astype(MXU_DTYPE), wd_ref[sl, :])
    o_ref[...] = _layer_norm_rows(ALPHA * h + acc, g_ref[...], b_ref[...])


def _ffn(h, wg, wu, wd, g, b):
    n = h.shape[0]
    tm = TM_PROJ
    row = pl.BlockSpec((tm, D_MODEL), lambda i: (i, 0))
    full = lambda a: pl.BlockSpec(a.shape, lambda i: (0, 0))
    return pl.pallas_call(
        _ffn_kernel,
        out_shape=jax.ShapeDtypeStruct((n, D_MODEL), jnp.float32),
        grid=(n // tm,),
        in_specs=[row, full(wg), full(wu), full(wd), full(g), full(b)],
        out_specs=row,
        compiler_params=_cparams(1),
        name="ffn_ln",
    )(h, wg, wu, wd, g, b)


def _rope_freq_row():
    def inv_freq(half):
        return jnp.power(ROPE_THETA, -jnp.arange(half, dtype=jnp.float32) / half)

    f8, g4 = inv_freq(HEAD_DIM // 8), inv_freq(IDX_DIM // 8)
    head = jnp.concatenate([f8, f8, jnp.zeros((HEAD_DIM - 16,), jnp.float32)])
    idx = jnp.concatenate([g4, g4, jnp.zeros((IDX_DIM - 8,), jnp.float32)])
    return jnp.concatenate([head, idx, idx])[None, :]


def _permute_in_proj(w, b):
    def cols(a):
        sl = lambda o, n: a[..., o:o + n]
        pad = jnp.zeros(a.shape[:-1] + (LANES - _F_NG - 3 * NSA_HEADS,), a.dtype)
        return jnp.concatenate([
            sl(_O_DQ, 512), sl(_O_NQ, 512), sl(_O_IQ, 256), sl(_O_KC, 256), sl(_O_KS, 256), sl(_O_KW, 256),
            sl(_O_DK, 128), sl(_O_IK, IDX_DIM), sl(_O_IW, IDX_HEADS), sl(_O_NG, 3 * NSA_HEADS), pad], axis=-1)

    return cols(w).astype(MXU_DTYPE), cols(b)[None, :]


def _selection_tables(s):
    n_chunk = s // CMP_STRIDE
    n_slc = s // SLC_BLOCK
    c = np.arange(n_chunk)[:, None]
    j = np.arange(LANES)[None, :]
    ov = ((c * CMP_STRIDE < j * SLC_BLOCK + SLC_BLOCK) & (c * CMP_STRIDE + CMP_BLOCK - 1 >= j * SLC_BLOCK)
          & (c < n_chunk - 1) & (j < n_slc))
    ov_p = np.zeros((LANES, LANES), np.float32)
    ov_p[:n_chunk] = ov
    kpos = np.arange(s).reshape(s // CH, 1, CH)
    ex = (kpos // SLC_BLOCK == np.arange(LANES)[None, :, None])
    tri = np.arange(CH)[:, None] < np.arange(CH)[None, :]
    return (jnp.asarray(ov_p, MXU_DTYPE), jnp.asarray(ex, MXU_DTYPE), jnp.asarray(tri, MXU_DTYPE))


def _layer(x, positions, w_in, b_in, cmp_pos, cmp_w1, cmp_w2, mix_gain, w_out,
           ln1_g, ln1_b, w_gate, w_up, w_down, ln2_g, ln2_b):
    b, s, d = x.shape
    assert d == D_MODEL and s % TQ == 0 and (b * s) % TM_PROJ == 0 and s // CMP_STRIDE <= LANES
    n = b * s
    x2 = x.reshape(n, d)
    w_p, b_p = _permute_in_proj(w_in, b_in)
    dq, nq, iq, kcf, vcf, ks, vs, kw, vw, dkv, f = _in_proj(x2, positions.reshape(n, 1), _rope_freq_row(), w_p, b_p)
    r3 = lambda a: a.reshape(b, s, a.shape[-1])
    ov, ex, tri = _selection_tables(s)
    kcv = _compress(r3(kcf), r3(vcf), cmp_pos, cmp_w1.astype(MXU_DTYPE), cmp_w2.astype(MXU_DTYPE))
    o_dsa = _dsa(r3(dq), r3(iq), r3(f), r3(dkv), tri, mix_gain[None, :DSA_WIDTH])
    o_nsa = _nsa(r3(nq), r3(f), kcv, r3(ks), r3(vs), r3(kw), r3(vw), ov, ex, mix_gain[None, DSA_WIDTH:])
    h = _out_proj(o_dsa.reshape(n, DSA_WIDTH), o_nsa.reshape(n, NSA_WIDTH), x2, w_out.astype(MXU_DTYPE),
                  ln1_g[None, :], ln1_b[None, :])
    y = _ffn(h, w_gate.astype(MXU_DTYPE), w_up.astype(MXU_DTYPE), w_down.astype(MXU_DTYPE),
             ln2_g[None, :], ln2_b[None, :])
    return y.reshape(b, s, d)


def kernel(x, positions, w_in, b_in, cmp_pos, cmp_w1, cmp_w2, mix_gain, w_out, ln1_g, ln1_b, w_gate, w_up, w_down, ln2_g, ln2_b):
    for l in range(DEPTH):
        x = _layer(x, positions, w_in[l], b_in[l], cmp_pos[l], cmp_w1[l], cmp_w2[l], mix_gain[l], w_out[l],
                   ln1_g[l], ln1_b[l], w_gate[l], w_up[l], w_down[l], ln2_g[l], ln2_b[l])
    return x
```

```python
import functools

import jax
import jax.numpy as jnp
import numpy as np
from jax import lax
from jax.experimental import pallas as pl
from jax.experimental.pallas import tpu as pltpu

D_MODEL = 1024
HEAD_DIM = 64
DSA_HEADS = 8
NSA_HEADS = 8
DSA_WIDTH = DSA_HEADS * HEAD_DIM
NSA_WIDTH = NSA_HEADS * HEAD_DIM
ROPE_THETA = 500000.0
IDX_HEADS = 8
IDX_DIM = 32
IDX_TOPK = 256
NSA_KV_GROUPS = 2
NSA_GROUP_SIZE = NSA_HEADS // NSA_KV_GROUPS
KV_WIDTH = NSA_KV_GROUPS * HEAD_DIM
CMP_BLOCK = 32
CMP_STRIDE = 16
CMP_HIDDEN = 2 * HEAD_DIM
SLC_BLOCK = 64
SLC_COUNT = 8
WINDOW = 512
FORCE_SCORE = 1.0e4
FFN_HIDDEN = 2816
DEPTH = 1
ALPHA = (2.0 * DEPTH) ** 0.25
LN_EPS = 1e-5
ATTN_SCALE = HEAD_DIM ** -0.5
IDX_SCALE = (IDX_HEADS ** -0.5) * (IDX_DIM ** -0.5)

LANES = 128
SUBLANES = 8
MXU_DTYPE = jnp.bfloat16
TQ = 128
CH = 256
TM_PROJ = 512
FFN_CHUNK = 256
VMEM_LIMIT = 56 * 1024 * 1024
NEG_BIG = -1.0e30

_O_DQ, _O_DK, _O_DV, _O_IQ, _O_IW, _O_IK, _O_NQ = 0, 512, 576, 640, 896, 904, 936
_O_KC, _O_VC, _O_KS, _O_VS, _O_KW, _O_VW, _O_NG = 1448, 1576, 1704, 1832, 1960, 2088, 2216

_N_NAT = 768
_T_DQ, _T_NQ, _T_IQ, _T_DV, _T_VS, _T_VW, _T_IW = 0, 512, 1024, 1280, 1344, 1472, 1600
_N_TR = 1632
_G_ROWS = IDX_HEADS + 3 * NSA_HEADS

_INT_MIN = -2 ** 31
_KEY_NEG_INF = int(np.array(-np.inf, np.float32).view(np.int32)) ^ 0x7FFFFFFF


def _cparams(n_axes):
    return pltpu.CompilerParams(dimension_semantics=("arbitrary",) * n_axes,
                                vmem_limit_bytes=VMEM_LIMIT)


def _dot(a, b):
    return jnp.dot(a, b, preferred_element_type=jnp.float32)


def _dot_nt(a, b):
    return lax.dot_general(a, b, (((1,), (1,)), ((), ())), preferred_element_type=jnp.float32)


def _dot_tn(a, b):
    return lax.dot_general(a, b, (((0,), (0,)), ((), ())), preferred_element_type=jnp.float32)


def _layer_norm_rows(v, g, b):
    mu = jnp.mean(v, axis=-1, keepdims=True)
    d = v - mu
    var = jnp.mean(d * d, axis=-1, keepdims=True)
    return d * lax.rsqrt(var + LN_EPS) * g + b


def _fold8(v, op):
    parts = [v[SUBLANES * r:SUBLANES * (r + 1)] for r in range(v.shape[0] // SUBLANES)]
    while len(parts) > 1:
        parts = [op(parts[r], parts[r + 1]) for r in range(0, len(parts) - 1, 2)] + parts[len(parts) & ~1:]
    return parts[0]


def _in_proj_kernel(x_ref, posc_ref, posr_ref, fr_ref, fc_ref, wn_ref, bn_ref, wt_ref, bt_ref,
                    dkv_ref, ks_ref, kw_ref, ikn_ref, kcf_ref, vcf_ref,
                    dqT_ref, nqT_ref, iqT_ref, gT_ref, dvT_ref, vsT_ref, vwT_ref):
    xb = x_ref[...].astype(MXU_DTYPE)
    tm = xb.shape[0]
    lane = lax.broadcasted_iota(jnp.int32, (tm, LANES), 1)
    ang = posc_ref[...].astype(jnp.float32) * fr_ref[...]
    cc, ss = jnp.cos(ang), jnp.sin(ang)
    cc_r, ss_r = pltpu.roll(cc, 64, 1), pltpu.roll(ss, 64, 1)
    lo = lane < 64
    c64, s64 = jnp.where(lo, cc, cc_r), jnp.where(lo, ss, ss_r)
    x1_64 = (lane % 64) < 8

    def rope64(y, c=c64, s=s64):
        p = jnp.where(x1_64, -pltpu.roll(y, LANES - 8, 1), pltpu.roll(y, 8, 1))
        return y * c + p * s

    def proj(col):
        return _dot(xb, wn_ref[:, col:col + 256]) + bn_ref[:, col:col + 256]

    y = proj(0)
    dkv_ref[...] = rope64(y[:, 0:128], jnp.where(lo, c64, 1.0), jnp.where(lo, s64, 0.0)).astype(dkv_ref.dtype)
    ks_ref[...] = rope64(y[:, 128:256]).astype(ks_ref.dtype)
    y = proj(256)
    kw_ref[...] = rope64(y[:, 0:128]).astype(kw_ref.dtype)
    is_ik = lane < IDX_DIM
    c32, s32 = jnp.where(is_ik, cc_r, 1.0), jnp.where(is_ik, ss_r, 0.0)
    f = y[:, 128:256]
    p = jnp.where(lane < 4, -pltpu.roll(f, LANES - 4, 1), pltpu.roll(f, 4, 1))
    ikn_ref[...] = (f * c32 + p * s32).astype(ikn_ref.dtype)
    y = proj(512)
    kcf_ref[...] = rope64(y[:, 0:128])
    vcf_ref[...] = y[:, 128:256]

    angT = fc_ref[:, 0:1] * posr_ref[...].astype(jnp.float32)
    cT, sT = jnp.cos(angT), jnp.sin(angT)
    c8, s8 = cT[0:8], sT[0:8]
    c4, s4 = cT[8:16], sT[8:16] * fc_ref[8:16, 1:2]

    def projT(r0, n):
        return _dot_nt(wt_ref[r0:r0 + n, :], xb) + bt_ref[r0:r0 + n, :]

    for q_ref, base in ((dqT_ref, _T_DQ), (nqT_ref, _T_NQ)):
        for blk in range(2):
            yT = projT(base + 256 * blk, 256)
            for h in range(4):
                r0 = HEAD_DIM * h
                x1, x2 = yT[r0:r0 + 8], yT[r0 + 8:r0 + 16]
                head = jnp.concatenate([x1 * c8 - x2 * s8, x2 * c8 + x1 * s8, yT[r0 + 16:r0 + HEAD_DIM]], axis=0)
                q_ref[256 * blk + r0:256 * blk + r0 + HEAD_DIM, :] = (head * ATTN_SCALE).astype(q_ref.dtype)
    yT = projT(_T_IQ, 256)
    for h in range(IDX_HEADS):
        r0 = IDX_DIM * h
        g8 = yT[r0:r0 + 8]
        head = jnp.concatenate([g8 * c4 + pltpu.roll(g8, 4, 0) * s4, yT[r0 + 8:r0 + IDX_DIM]], axis=0)
        iqT_ref[r0:r0 + IDX_DIM, :] = head.astype(iqT_ref.dtype)
    yT = projT(_T_DV, 320)
    for c in range(tm // CH):
        cs = slice(CH * c, CH * (c + 1))
        dvT_ref[c] = yT[0:64, cs].astype(dvT_ref.dtype)
        vsT_ref[c] = yT[64:192, cs].astype(vsT_ref.dtype)
        vwT_ref[c] = yT[192:320, cs].astype(vwT_ref.dtype)
    yT = projT(_T_IW, _G_ROWS)
    rowi = lax.broadcasted_iota(jnp.int32, yT.shape, 0)
    gT_ref[...] = jnp.where(rowi >= IDX_HEADS, jax.nn.sigmoid(yT), yT)


def _in_proj(x, posc, posr, fr, fc, wn, bn, wt, bt):
    b, s, _ = x.shape
    tm = TM_PROJ
    bf, f32 = MXU_DTYPE, jnp.float32
    nat = lambda w: pl.BlockSpec((None, tm, w), lambda bi, j: (bi, j, 0))
    tr = lambda r: pl.BlockSpec((None, r, tm), lambda bi, j: (bi, 0, j))
    trc = lambda r: pl.BlockSpec((None, tm // CH, r, CH), lambda bi, j: (bi, j, 0, 0))
    full = lambda a: pl.BlockSpec(a.shape, lambda bi, j: (0, 0))
    sd = jax.ShapeDtypeStruct
    out_shape = [sd((b, s, 128), bf), sd((b, s, 128), bf), sd((b, s, 128), bf), sd((b, s, 128), bf),
                 sd((b, s, 128), f32), sd((b, s, 128), f32),
                 sd((b, DSA_WIDTH, s), bf), sd((b, NSA_WIDTH, s), bf), sd((b, IDX_HEADS * IDX_DIM, s), bf),
                 sd((b, _G_ROWS, s), f32),
                 sd((b, s // CH, HEAD_DIM, CH), bf), sd((b, s // CH, KV_WIDTH, CH), bf), sd((b, s // CH, KV_WIDTH, CH), bf)]
    out_specs = [nat(128)] * 6 + [tr(DSA_WIDTH), tr(NSA_WIDTH), tr(IDX_HEADS * IDX_DIM), tr(_G_ROWS),
                                  trc(HEAD_DIM), trc(KV_WIDTH), trc(KV_WIDTH)]
    return pl.pallas_call(
        _in_proj_kernel,
        out_shape=out_shape,
        grid=(b, s // tm),
        in_specs=[nat(D_MODEL), nat(1), pl.BlockSpec((None, 1, tm), lambda bi, j: (bi, 0, j)),
                  full(fr), full(fc), full(wn), full(bn), full(wt), full(bt)],
        out_specs=out_specs,
        compiler_params=_cparams(2),
        name="in_proj",
    )(x, posc, posr, fr, fc, wn, bn, wt, bt)


def _compress_kernel(kc_ref, vc_ref, pos_ref, w1_ref, w2_ref, kcn_ref, vcT_ref):
    s = kc_ref.shape[0]
    n_chunk = s // CMP_STRIDE
    if n_chunk < LANES:
        kcn_ref[...] = jnp.zeros(kcn_ref.shape, kcn_ref.dtype)
        vcT_ref[...] = jnp.zeros(vcT_ref.shape, vcT_ref.dtype)
    for kv, src_ref in enumerate((kc_ref, vc_ref)):
        top = [jnp.zeros((n_chunk, CMP_HIDDEN), jnp.float32) for _ in range(NSA_KV_GROUPS)]
        bot = [jnp.zeros((n_chunk, CMP_HIDDEN), jnp.float32) for _ in range(NSA_KV_GROUPS)]
        for l in range(CMP_STRIDE):
            rows_all = src_ref[pl.ds(l, n_chunk, stride=CMP_STRIDE), :]
            w_top = w1_ref[kv, l * HEAD_DIM:(l + 1) * HEAD_DIM, :]
            w_bot = w1_ref[kv, (CMP_STRIDE + l) * HEAD_DIM:(CMP_STRIDE + l + 1) * HEAD_DIM, :]
            for g in range(NSA_KV_GROUPS):
                rows = rows_all[:, HEAD_DIM * g:HEAD_DIM * (g + 1)]
                a = (rows + pos_ref[kv, l:l + 1, :]).astype(MXU_DTYPE)
                bm = (rows + pos_ref[kv, CMP_STRIDE + l:CMP_STRIDE + l + 1, :]).astype(MXU_DTYPE)
                top[g] = top[g] + _dot(a, w_top)
                bot[g] = bot[g] + _dot(bm, w_bot)
        for g in range(NSA_KV_GROUPS):
            h = top[g] + pltpu.roll(bot[g], n_chunk - 1, 0)
            y = _dot(jax.nn.gelu(h).astype(MXU_DTYPE), w2_ref[kv])
            ridx = lax.broadcasted_iota(jnp.int32, y.shape, 0)
            y = jnp.where(ridx < n_chunk - 1, y, 0.0)
            if kv == 0:
                kcn_ref[0:n_chunk, HEAD_DIM * g:HEAD_DIM * (g + 1)] = y.astype(kcn_ref.dtype)
            else:
                vcT_ref[HEAD_DIM * g:HEAD_DIM * (g + 1), 0:n_chunk] = y.T.astype(vcT_ref.dtype)


def _compress(kc3, vc3, cmp_pos, w1, w2):
    b, s, _ = kc3.shape
    src = pl.BlockSpec((None, s, KV_WIDTH), lambda i: (i, 0, 0))
    out = pl.BlockSpec((None, LANES, KV_WIDTH), lambda i: (i, 0, 0))
    return pl.pallas_call(
        _compress_kernel,
        out_shape=[jax.ShapeDtypeStruct((b, LANES, KV_WIDTH), MXU_DTYPE)] * 2,
        grid=(b,),
        in_specs=[src, src,
                  pl.BlockSpec(cmp_pos.shape, lambda i: (0, 0, 0)),
                  pl.BlockSpec(w1.shape, lambda i: (0, 0, 0)),
                  pl.BlockSpec(w2.shape, lambda i: (0, 0, 0))],
        out_specs=[out, out],
        compiler_params=_cparams(1),
        name="nsa_compress",
    )(kc3, vc3, cmp_pos, w1, w2)


def _stack_heads(qT, h0, nh, place, n_place):
    q = jnp.concatenate([qT[HEAD_DIM * (h0 + r):HEAD_DIM * (h0 + r + 1), :] for r in range(nh)], axis=1)
    z = jnp.zeros_like(q)
    return jnp.concatenate([q if p == place else z for p in range(n_place)], axis=0)


def _flash_attention(q_pad, nh, k_ref, vT_ref, v_r0, c_lo, c_hi, mask_fn):
    w = nh * TQ

    def body(c, carry):
        m, l8, acc = carry
        off = pl.multiple_of(c * CH, CH)
        sT = _dot(k_ref[pl.ds(off, CH), :], q_pad)
        msk = mask_fn(c)
        m_new, alpha, l_new, p_blocks = [], [], [], []
        for h in range(nh):
            hs = slice(TQ * h, TQ * (h + 1))
            blk = jnp.where(msk, sT[:, hs], -jnp.inf)
            mh = jnp.maximum(m[:, hs], jnp.max(_fold8(blk, jnp.maximum), axis=0, keepdims=True))
            ah = jnp.exp(m[:, hs] - mh)
            ph = jnp.exp(blk - mh)
            m_new.append(mh)
            alpha.append(ah)
            l_new.append(l8[:, hs] * ah + _fold8(ph, jnp.add))
            p_blocks.append(ph.astype(MXU_DTYPE))
        p = jnp.concatenate(p_blocks, axis=1)
        vT = vT_ref[c][v_r0:v_r0 + HEAD_DIM, :]
        acc = acc * jnp.concatenate(alpha, axis=1) + _dot(vT, p)
        return jnp.concatenate(m_new, axis=1), jnp.concatenate(l_new, axis=1), acc

    init = (jnp.full((1, w), NEG_BIG, jnp.float32), jnp.zeros((SUBLANES, w), jnp.float32),
            jnp.zeros((HEAD_DIM, w), jnp.float32))
    _, l8, acc = lax.fori_loop(c_lo, c_hi, body, init)
    l = jnp.sum(l8, axis=0, keepdims=True)
    return acc / jnp.maximum(l, 1e-30)


def _group_rms_store(heads, gain_ref, o_ref):
    o = jnp.concatenate(heads, axis=0)
    ms = jnp.mean(o * o, axis=0, keepdims=True)
    o_ref[...] = (o * lax.rsqrt(ms + LN_EPS) * gain_ref[...]).astype(o_ref.dtype)


def _sortable_key(v):
    bits = lax.bitcast_convert_type(v + 0.0, jnp.int32)
    return jnp.where(bits < 0, bits ^ 0x7FFFFFFF, bits)


def _dsa_kernel(topk, dqT_ref, iqT_ref, gT_ref, dkv_ref, ikn_ref, dvT_ref, tri_ref, gain_ref, o_ref, key_scr):
    i = pl.program_id(1)
    t0 = i * TQ
    nk = (t0 + TQ - 1) // CH + 1
    krow = lax.broadcasted_iota(jnp.int32, (CH, TQ), 0)
    qpos = t0 + lax.broadcasted_iota(jnp.int32, (CH, TQ), 1)

    iq_pad = _stack_heads_idx(iqT_ref[...])
    w_row = jnp.concatenate([gT_ref[h:h + 1, :] for h in range(IDX_HEADS)], axis=1)

    def index_body(c, carry):
        off = pl.multiple_of(c * CH, CH)
        r = jnp.maximum(_dot(ikn_ref[pl.ds(off, CH), :], iq_pad), 0.0) * w_row
        acc = r[:, 0:TQ]
        for h in range(1, IDX_HEADS):
            acc = acc + r[:, TQ * h:TQ * (h + 1)]
        idx = jnp.where(off + krow <= qpos, acc * IDX_SCALE, -jnp.inf)
        key_scr[c] = _sortable_key(idx)
        return carry

    lax.fori_loop(0, nk, index_body, 0)

    def count_ge(cand):
        cb = jnp.broadcast_to(cand, (CH, TQ))

        def body(c, acc):
            return acc + _fold8(jnp.where(key_scr[c] >= cb, 1.0, 0.0), jnp.add)

        part = lax.fori_loop(0, nk, body, jnp.zeros((SUBLANES, TQ), jnp.float32))
        return jnp.sum(part, axis=0, keepdims=True)

    kf = jnp.float32(topk)
    n_all = (nk * CH).astype(jnp.float32)
    lo0 = jnp.full((1, TQ), _INT_MIN, jnp.int32)
    cnt0 = jnp.zeros((1, TQ), jnp.float32) + n_all
    cand = jnp.zeros((1, TQ), jnp.int32)
    cnt = count_ge(cand)
    ok = cnt >= kf
    lo0, cnt0 = jnp.where(ok, cand, lo0), jnp.where(ok, cnt, cnt0)

    def bit_body(t, carry):
        lo, cnt_lo = carry
        cand = lo + jnp.left_shift(jnp.int32(1), 30 - t)
        cnt = count_ge(cand)
        ok = cnt >= kf
        return jnp.where(ok, cand, lo), jnp.where(ok, cnt, cnt_lo)

    tau, cnt_ge_tau = lax.fori_loop(0, 31, bit_body, (lo0, cnt0))

    tie_q = (cnt_ge_tau > kf) & (tau > _KEY_NEG_INF)

    @pl.when(jnp.max(jnp.where(tie_q, 1.0, 0.0)) > 0.0)
    def _():
        need = kf - count_ge(tau + 1)
        taub = jnp.broadcast_to(tau, (CH, TQ))
        live = jnp.broadcast_to(tie_q, (CH, TQ))

        def tie_body(c, seen):
            k = key_scr[c]
            tie = (k == taub) & live
            tf = jnp.where(tie, 1.0, 0.0)
            before = seen + _dot(tri_ref[...], tf.astype(MXU_DTYPE))
            key_scr[c] = jnp.where(tie & (before >= need), taub - 1, k)
            return seen + jnp.sum(_fold8(tf, jnp.add), axis=0, keepdims=True)

        lax.fori_loop(0, nk, tie_body, jnp.zeros((1, TQ), jnp.float32))

    taub = jnp.broadcast_to(tau, (CH, TQ))

    def mask_fn(c):
        return (key_scr[c] >= taub) & (c * CH + krow <= qpos)

    q_pad = _stack_heads(dqT_ref[...], 0, DSA_HEADS, 0, 2)
    oT = _flash_attention(q_pad, DSA_HEADS, dkv_ref, dvT_ref, 0, 0, nk, mask_fn)
    _group_rms_store([oT[:, TQ * h:TQ * (h + 1)] for h in range(DSA_HEADS)], gain_ref, o_ref)


def _stack_heads_idx(iqT):
    q = jnp.concatenate([iqT[IDX_DIM * h:IDX_DIM * (h + 1), :] for h in range(IDX_HEADS)], axis=1)
    return jnp.concatenate([q, jnp.zeros((LANES - IDX_DIM, q.shape[1]), q.dtype)], axis=0)


def _dsa(dqT, iqT, gT, dkv, ikn, dvT, tri, gain):
    b, _, s = dqT.shape
    topk = min(IDX_TOPK, s // 4)
    qblk = lambda r: pl.BlockSpec((None, r, TQ), lambda bi, i: (bi, 0, i))
    sblk = lambda w: pl.BlockSpec((None, s, w), lambda bi, i: (bi, 0, 0))
    const = lambda a: pl.BlockSpec(a.shape, lambda bi, i: (0,) * a.ndim)
    return pl.pallas_call(
        functools.partial(_dsa_kernel, topk),
        out_shape=jax.ShapeDtypeStruct((b, DSA_WIDTH, s), MXU_DTYPE),
        grid=(b, s // TQ),
        in_specs=[qblk(DSA_WIDTH), qblk(IDX_HEADS * IDX_DIM), qblk(_G_ROWS), sblk(LANES), sblk(LANES),
                  pl.BlockSpec((None, s // CH, HEAD_DIM, CH), lambda bi, i: (bi, 0, 0, 0)),
                  const(tri), const(gain)],
        out_specs=qblk(DSA_WIDTH),
        scratch_shapes=[pltpu.VMEM((s // CH, CH, TQ), jnp.int32)],
        compiler_params=_cparams(2),
        name="dsa_attention",
    )(dqT, iqT, gT, dkv, ikn, dvT, tri, gain)


def _split3(v):
    a = v.astype(MXU_DTYPE)
    r = v - a.astype(jnp.float32)
    b = r.astype(MXU_DTYPE)
    c = (r - b.astype(jnp.float32)).astype(MXU_DTYPE)
    return a, b, c


def _nsa_kernel(n_cmp, n_slc, n_sel, nqT_ref, gT_ref, kcn_ref, vcT_ref, ks_ref, vsT_ref, kw_ref, vwT_ref,
                ovT_ref, exT_ref, gain_ref, o_ref):
    i = pl.program_id(1)
    t0 = i * TQ
    krow = lax.broadcasted_iota(jnp.int32, (CH, TQ), 0)
    qpos = t0 + lax.broadcasted_iota(jnp.int32, (CH, TQ), 1)
    crow = lax.broadcasted_iota(jnp.int32, (LANES, TQ), 0)
    qpos_c = t0 + lax.broadcasted_iota(jnp.int32, (LANES, TQ), 1)
    jrow = lax.broadcasted_iota(jnp.int32, (n_slc, TQ), 0)
    qpos_j = t0 + lax.broadcasted_iota(jnp.int32, (n_slc, TQ), 1)
    nqT = nqT_ref[...]
    R = NSA_GROUP_SIZE
    heads_out = []
    for g in range(NSA_KV_GROUPS):
        q_pad = _stack_heads(nqT, R * g, R, g, NSA_KV_GROUPS)
        sc = _dot(kcn_ref[...], q_pad)
        cmask = (CMP_STRIDE * crow + CMP_BLOCK - 1 <= qpos_c) & (crow < n_cmp)
        p_blocks, psum = [], None
        for r in range(R):
            blk = jnp.where(cmask, sc[:, TQ * r:TQ * (r + 1)], -jnp.inf)
            m = jnp.max(blk, axis=0, keepdims=True)
            m = jnp.where(m > -jnp.inf, m, 0.0)
            p = jnp.exp(blk - m)
            p = p / jnp.maximum(jnp.sum(p, axis=0, keepdims=True), 1e-30)
            p_blocks.append(p.astype(MXU_DTYPE))
            psum = p if psum is None else psum + p
        o_c = _dot(vcT_ref[HEAD_DIM * g:HEAD_DIM * (g + 1), :], jnp.concatenate(p_blocks, axis=1))
        pa, pb, pc = _split3(psum)
        ov = ovT_ref[0:n_slc, :]
        imp = _dot(ov, pa) + _dot(ov, pb) + _dot(ov, pc)
        cur = qpos_j // SLC_BLOCK
        forced = (jrow == 0) | (jrow == cur) | (jrow == cur - 1)
        visible = SLC_BLOCK * jrow <= qpos_j
        work = jnp.where(visible, jnp.where(forced, FORCE_SCORE, imp), -1.0)
        rank = jnp.zeros(work.shape, jnp.float32)
        for j in range(n_slc):
            wj = jnp.broadcast_to(work[j:j + 1, :], work.shape)
            ahead = (wj > work) | ((wj == work) & (jrow > j))
            rank = rank + jnp.where(ahead, 1.0, 0.0)
        sel = jnp.where((rank < n_sel) & visible, 1.0, 0.0)
        sel_pad = jnp.concatenate([sel, jnp.zeros((LANES - n_slc, TQ), jnp.float32)], axis=0).astype(MXU_DTYPE)

        def slc_mask(c):
            return (_dot(exT_ref[c], sel_pad) > 0.5) & (c * CH + krow <= qpos)

        o_s = _flash_attention(q_pad, R, ks_ref, vsT_ref, HEAD_DIM * g, 0, (t0 + TQ - 1) // CH + 1, slc_mask)

        def win_mask(c):
            d = qpos - (c * CH + krow)
            return (d >= 0) & (d < WINDOW)

        o_w = _flash_attention(q_pad, R, kw_ref, vwT_ref, HEAD_DIM * g,
                               jnp.maximum(t0 - WINDOW + 1, 0) // CH, (t0 + TQ - 1) // CH + 1, win_mask)
        for r in range(R):
            gr = IDX_HEADS + 3 * (R * g + r)
            hs = slice(TQ * r, TQ * (r + 1))
            heads_out.append(gT_ref[gr:gr + 1, :] * o_c[:, hs] + gT_ref[gr + 1:gr + 2, :] * o_s[:, hs]
                             + gT_ref[gr + 2:gr + 3, :] * o_w[:, hs])
    _group_rms_store(heads_out, gain_ref, o_ref)


def _nsa(nqT, gT, kcn, vcT, ks, vsT, kw, vwT, ovT, exT, gain):
    b, _, s = nqT.shape
    n_slc = s // SLC_BLOCK
    n_sel = min(SLC_COUNT, n_slc)
    qblk = lambda r: pl.BlockSpec((None, r, TQ), lambda bi, i: (bi, 0, i))
    bfull = lambda a: pl.BlockSpec((None,) + a.shape[1:], lambda bi, i: (bi,) + (0,) * (a.ndim - 1))
    const = lambda a: pl.BlockSpec(a.shape, lambda bi, i: (0,) * a.ndim)
    return pl.pallas_call(
        functools.partial(_nsa_kernel, s // CMP_STRIDE - 1, n_slc, n_sel),
        out_shape=jax.ShapeDtypeStruct((b, NSA_WIDTH, s), MXU_DTYPE),
        grid=(b, s // TQ),
        in_specs=[qblk(NSA_WIDTH), qblk(_G_ROWS), bfull(kcn), bfull(vcT), bfull(ks), bfull(vsT), bfull(kw), bfull(vwT),
                  const(ovT), const(exT), const(gain)],
        out_specs=qblk(NSA_WIDTH),
        compiler_params=_cparams(2),
        name="nsa_attention",
    )(nqT, gT, kcn, vcT, ks, vsT, kw, vwT, ovT, exT, gain)


def _out_proj_kernel(odT_ref, onT_ref, x_ref, w_ref, g_ref, b_ref, h_ref):
    y = _dot_tn(odT_ref[...], w_ref[0:DSA_WIDTH, :]) + _dot_tn(onT_ref[...], w_ref[DSA_WIDTH:, :])
    h_ref[...] = _layer_norm_rows(ALPHA * x_ref[...] + y, g_ref[...], b_ref[...])


def _out_proj(odT, onT, x, w, g, b_):
    b, s, _ = x.shape
    tm = TM_PROJ
    tr = lambda r: pl.BlockSpec((None, r, tm), lambda bi, j: (bi, 0, j))
    nat = pl.BlockSpec((None, tm, D_MODEL), lambda bi, j: (bi, j, 0))
    full = lambda a: pl.BlockSpec(a.shape, lambda bi, j: (0, 0))
    return pl.pallas_call(
        _out_proj_kernel,
        out_shape=jax.ShapeDtypeStruct((b, s, D_MODEL), jnp.float32),
        grid=(b, s // tm),
        in_specs=[tr(DSA_WIDTH), tr(NSA_WIDTH), nat, full(w), full(g), full(b_)],
        out_specs=nat,
        compiler_params=_cparams(2),
        name="out_proj_ln",
    )(odT, onT, x, w, g, b_)


def _ffn_kernel(h_ref, wg_ref, wu_ref, wd_ref, g_ref, b_ref, o_ref):
    h = h_ref[...]
    hb = h.astype(MXU_DTYPE)
    acc = jnp.zeros(h.shape, jnp.float32)
    for c in range(FFN_HIDDEN // FFN_CHUNK):
        sl = slice(FFN_CHUNK * c, FFN_CHUNK * (c + 1))
        gate = _dot(hb, wg_ref[:, sl])
        up = _dot(hb, wu_ref[:, sl])
        acc = acc + _dot((jax.nn.silu(gate) * up).astype(MXU_DTYPE), wd_ref[sl, :])
    o_ref[...] = _layer_norm_rows(ALPHA * h + acc, g_ref[...], b_ref[...])


def _ffn(h, wg, wu, wd, g, b):
    n = h.shape[0]
    tm = TM_PROJ
    row = pl.BlockSpec((tm, D_MODEL), lambda i: (i, 0))
    full = lambda a: pl.BlockSpec(a.shape, lambda i: (0, 0))
    return pl.pallas_call(
        _ffn_kernel,
        out_shape=jax.ShapeDtypeStruct((n, D_MODEL), jnp.float32),
        grid=(n // tm,),
        in_specs=[row, full(wg), full(wu), full(wd), full(g), full(b)],
        out_specs=row,
        compiler_params=_cparams(1),
        name="ffn_ln",
    )(h, wg, wu, wd, g, b)


def _rope_tables():
    def inv_freq(half):
        return jnp.power(ROPE_THETA, -jnp.arange(half, dtype=jnp.float32) / half)

    f8, g4 = inv_freq(HEAD_DIM // 8), inv_freq(IDX_DIM // 8)
    head = jnp.concatenate([f8, f8, jnp.zeros((HEAD_DIM - 16,), jnp.float32)])
    idx = jnp.concatenate([g4, g4, jnp.zeros((IDX_DIM - 8,), jnp.float32)])
    fr = jnp.concatenate([head, idx, idx])[None, :]
    sign = jnp.concatenate([jnp.zeros((8,)), -jnp.ones((4,)), jnp.ones((4,))]).astype(jnp.float32)
    fc = jnp.stack([jnp.concatenate([f8, g4, g4]), sign], axis=1)
    return fr, jnp.pad(fc, ((0, 0), (0, LANES - 2)))


def _split_in_proj(w, b):
    sl = lambda a, o, n: a[..., o:o + n]
    zpad = lambda a, n: jnp.zeros(a.shape[:-1] + (n,), a.dtype)

    def nat(a):
        return jnp.concatenate([sl(a, _O_DK, 128), sl(a, _O_KS, 128), sl(a, _O_KW, 128), sl(a, _O_IK, IDX_DIM),
                                zpad(a, LANES - IDX_DIM), sl(a, _O_KC, 128), sl(a, _O_VC, 128)], axis=-1)

    def tr(a):
        return jnp.concatenate([sl(a, _O_DQ, 512), sl(a, _O_NQ, 512), sl(a, _O_IQ, 256), sl(a, _O_DV, 64),
                                sl(a, _O_VS, 128), sl(a, _O_VW, 128), sl(a, _O_IW, IDX_HEADS),
                                sl(a, _O_NG, 3 * NSA_HEADS)], axis=-1)

    return nat(w).astype(MXU_DTYPE), nat(b)[None, :], tr(w).T.astype(MXU_DTYPE), tr(b)[:, None]


def _selection_tables(s):
    n_chunk = s // CMP_STRIDE
    n_slc = s // SLC_BLOCK
    j = np.arange(LANES)[:, None]
    c = np.arange(LANES)[None, :]
    ovT = ((c * CMP_STRIDE < j * SLC_BLOCK + SLC_BLOCK) & (c * CMP_STRIDE + CMP_BLOCK - 1 >= j * SLC_BLOCK)
           & (c < n_chunk - 1) & (j < n_slc))
    kpos = np.arange(s).reshape(s // CH, CH, 1)
    exT = (kpos // SLC_BLOCK == np.arange(LANES)[None, None, :])
    tri = np.arange(CH)[None, :] < np.arange(CH)[:, None]
    as_mxu = lambda a: jnp.asarray(a.astype(np.float32), MXU_DTYPE)
    return as_mxu(ovT), as_mxu(exT), as_mxu(tri)


def _layer(x, positions, w_in, b_in, cmp_pos, cmp_w1, cmp_w2, mix_gain, w_out,
           ln1_g, ln1_b, w_gate, w_up, w_down, ln2_g, ln2_b):
    b, s, d = x.shape
    assert d == D_MODEL and s % TM_PROJ == 0 and TM_PROJ % CH == 0 and CH % TQ == 0 and TQ == LANES
    assert s // CMP_STRIDE <= LANES and s // SLC_BLOCK <= LANES
    n = b * s
    fr, fc = _rope_tables()
    wn, bn, wt, bt = _split_in_proj(w_in, b_in)
    (dkv, ks, kw, ikn, kcf, vcf, dqT, nqT, iqT, gT, dvT, vsT, vwT) = _in_proj(
        x, positions[:, :, None], positions[:, None, :], fr, fc, wn, bn, wt, bt)
    ovT, exT, tri = _selection_tables(s)
    kcn, vcT = _compress(kcf, vcf, cmp_pos, cmp_w1.astype(MXU_DTYPE), cmp_w2.astype(MXU_DTYPE))
    gain = jnp.broadcast_to(mix_gain[:, None], (D_MODEL, LANES))
    o_dsa = _dsa(dqT, iqT, gT, dkv, ikn, dvT, tri, gain[:DSA_WIDTH])
    o_nsa = _nsa(nqT, gT, kcn, vcT, ks, vsT, kw, vwT, ovT, exT, gain[DSA_WIDTH:])
    h = _out_proj(o_dsa, o_nsa, x, w_out.astype(MXU_DTYPE), ln1_g[None, :], ln1_b[None, :])
    y = _ffn(h.reshape(n, d), w_gate.astype(MXU_DTYPE), w_up.astype(MXU_DTYPE), w_down.astype(MXU_DTYPE),
             ln2_g[None, :], ln2_b[None, :])
    return y.reshape(b, s, d)


def kernel(x, positions, w_in, b_in, cmp_pos, cmp_w1, cmp_w2, mix_gain, w_out, ln1_g, ln1_b, w_gate, w_up, w_down, ln2_g, ln2_b):
    for l in range(DEPTH):
        x = _layer(x, positions, w_in[l], b_in[l], cmp_pos[l], cmp_w1[l], cmp_w2[l], mix_gain[l], w_out[l],
                   ln1_g[l], ln1_b[l], w_gate[l], w_up[l], w_down[l], ln2_g[l], ln2_b[l])
    return x
```

```python
import functools

import jax
import jax.numpy as jnp
import numpy as np
from jax import lax
from jax.experimental import pallas as pl
from jax.experimental.pallas import tpu as pltpu

D_MODEL = 1024
HEAD_DIM = 64
DSA_HEADS = 8
NSA_HEADS = 8
DSA_WIDTH = DSA_HEADS * HEAD_DIM
NSA_WIDTH = NSA_HEADS * HEAD_DIM
ROPE_THETA = 500000.0
IDX_HEADS = 8
IDX_DIM = 32
IDX_TOPK = 256
NSA_KV_GROUPS = 2
NSA_GROUP_SIZE = NSA_HEADS // NSA_KV_GROUPS
KV_WIDTH = NSA_KV_GROUPS * HEAD_DIM
CMP_BLOCK = 32
CMP_STRIDE = 16
CMP_HIDDEN = 2 * HEAD_DIM
SLC_BLOCK = 64
SLC_COUNT = 8
WINDOW = 512
FORCE_SCORE = 1.0e4
FFN_HIDDEN = 2816
DEPTH = 1
ALPHA = (2.0 * DEPTH) ** 0.25
LN_EPS = 1e-5
Q_SCALE = HEAD_DIM ** -0.5 * 1.4426950408889634
IDX_SCALE = (IDX_HEADS ** -0.5) * (IDX_DIM ** -0.5)

LANES = 128
SUBLANES = 8
MXU_DTYPE = jnp.bfloat16
TQ = 128
CH = 256
TM_PROJ = 512
FFN_CHUNK = 256
VMEM_LIMIT = 56 * 1024 * 1024
NEG_BIG = -1.0e30
PACKED_ROWS = 16
HALF_MIN = -2 ** 15

_O_DQ, _O_DK, _O_DV, _O_IQ, _O_IW, _O_IK, _O_NQ = 0, 512, 576, 640, 896, 904, 936
_O_KC, _O_VC, _O_KS, _O_VS, _O_KW, _O_VW, _O_NG = 1448, 1576, 1704, 1832, 1960, 2088, 2216

_N_NAT = 768
_T_DQ, _T_NQ, _T_IQ, _T_DV, _T_VS, _T_VW, _T_IW = 0, 512, 1024, 1280, 1344, 1472, 1600
_N_TR = 1632
_G_ROWS = IDX_HEADS + 3 * NSA_HEADS

_INT_MIN = -2 ** 31
_KEY_NEG_INF = int(np.array(-np.inf, np.float32).view(np.int32)) ^ 0x7FFFFFFF


def _cparams(n_axes):
    return pltpu.CompilerParams(dimension_semantics=("arbitrary",) * n_axes,
                                vmem_limit_bytes=VMEM_LIMIT)


def _dot(a, b):
    return jnp.dot(a, b, preferred_element_type=jnp.float32)


def _dot_nt(a, b):
    return lax.dot_general(a, b, (((1,), (1,)), ((), ())), preferred_element_type=jnp.float32)


def _dot_tn(a, b):
    return lax.dot_general(a, b, (((0,), (0,)), ((), ())), preferred_element_type=jnp.float32)


def _layer_norm_rows(v, g, b):
    mu = jnp.mean(v, axis=-1, keepdims=True)
    d = v - mu
    var = jnp.mean(d * d, axis=-1, keepdims=True)
    return d * lax.rsqrt(var + LN_EPS) * g + b


def _fold_rows(v, rows, op=jnp.add):
    parts = [v[rows * r:rows * (r + 1)] for r in range(v.shape[0] // rows)]
    while len(parts) > 1:
        parts = [op(parts[r], parts[r + 1]) for r in range(0, len(parts) - 1, 2)] + parts[len(parts) & ~1:]
    return parts[0]


def _fold8(v, op):
    return _fold_rows(v, SUBLANES, op)


def _in_proj_kernel(x_ref, posc_ref, posr_ref, fr_ref, fc_ref, wn_ref, bn_ref, wt_ref, bt_ref,
                    dkv_ref, ks_ref, kw_ref, ikn_ref, kcf_ref, vcf_ref,
                    dqT_ref, nqT_ref, iqT_ref, gT_ref, dvT_ref, vsT_ref, vwT_ref):
    xb = x_ref[...].astype(MXU_DTYPE)
    tm = xb.shape[0]
    lane = lax.broadcasted_iota(jnp.int32, (tm, LANES), 1)
    ang = posc_ref[...].astype(jnp.float32) * fr_ref[...]
    cc, ss = jnp.cos(ang), jnp.sin(ang)
    cc_r, ss_r = pltpu.roll(cc, 64, 1), pltpu.roll(ss, 64, 1)
    lo = lane < 64
    c64, s64 = jnp.where(lo, cc, cc_r), jnp.where(lo, ss, ss_r)
    x1_64 = (lane % 64) < 8

    def rope64(y, c=c64, s=s64):
        p = jnp.where(x1_64, -pltpu.roll(y, LANES - 8, 1), pltpu.roll(y, 8, 1))
        return y * c + p * s

    def proj(col):
        return _dot(xb, wn_ref[:, col:col + 256]) + bn_ref[:, col:col + 256]

    y = proj(0)
    dkv_ref[...] = rope64(y[:, 0:128], jnp.where(lo, c64, 1.0), jnp.where(lo, s64, 0.0)).astype(dkv_ref.dtype)
    ks_ref[...] = rope64(y[:, 128:256]).astype(ks_ref.dtype)
    y = proj(256)
    kw_ref[...] = rope64(y[:, 0:128]).astype(kw_ref.dtype)
    is_ik = lane < IDX_DIM
    c32, s32 = jnp.where(is_ik, cc_r, 1.0), jnp.where(is_ik, ss_r, 0.0)
    f = y[:, 128:256]
    p = jnp.where(lane < 4, -pltpu.roll(f, LANES - 4, 1), pltpu.roll(f, 4, 1))
    ikn_ref[...] = (f * c32 + p * s32).astype(ikn_ref.dtype)
    y = proj(512)
    kcf_ref[...] = rope64(y[:, 0:128])
    vcf_ref[...] = y[:, 128:256]

    angT = fc_ref[:, 0:1] * posr_ref[...].astype(jnp.float32)
    cT, sT = jnp.cos(angT), jnp.sin(angT)
    c8, s8 = cT[0:8], sT[0:8]
    c4, s4 = cT[8:16], sT[8:16] * fc_ref[8:16, 1:2]

    def projT(r0, n):
        return _dot_nt(wt_ref[r0:r0 + n, :], xb) + bt_ref[r0:r0 + n, :]

    for q_ref, base in ((dqT_ref, _T_DQ), (nqT_ref, _T_NQ)):
        for blk in range(2):
            yT = projT(base + 256 * blk, 256)
            for h in range(4):
                r0 = HEAD_DIM * h
                x1, x2 = yT[r0:r0 + 8], yT[r0 + 8:r0 + 16]
                head = jnp.concatenate([x1 * c8 - x2 * s8, x2 * c8 + x1 * s8, yT[r0 + 16:r0 + HEAD_DIM]], axis=0)
                q_ref[256 * blk + r0:256 * blk + r0 + HEAD_DIM, :] = (head * Q_SCALE).astype(q_ref.dtype)
    yT = projT(_T_IQ, 256)
    for h in range(IDX_HEADS):
        r0 = IDX_DIM * h
        g8 = yT[r0:r0 + 8]
        head = jnp.concatenate([g8 * c4 + pltpu.roll(g8, 4, 0) * s4, yT[r0 + 8:r0 + IDX_DIM]], axis=0)
        iqT_ref[r0:r0 + IDX_DIM, :] = head.astype(iqT_ref.dtype)
    yT = projT(_T_DV, 320)
    for c in range(tm // CH):
        cs = slice(CH * c, CH * (c + 1))
        dvT_ref[c] = yT[0:64, cs].astype(dvT_ref.dtype)
        vsT_ref[c] = yT[64:192, cs].astype(vsT_ref.dtype)
        vwT_ref[c] = yT[192:320, cs].astype(vwT_ref.dtype)
    yT = projT(_T_IW, _G_ROWS)
    rowi = lax.broadcasted_iota(jnp.int32, yT.shape, 0)
    gT_ref[...] = jnp.where(rowi >= IDX_HEADS, jax.nn.sigmoid(yT), yT)


def _in_proj(x, posc, posr, fr, fc, wn, bn, wt, bt):
    b, s, _ = x.shape
    tm = TM_PROJ
    bf, f32 = MXU_DTYPE, jnp.float32
    nat = lambda w: pl.BlockSpec((None, tm, w), lambda bi, j: (bi, j, 0))
    tr = lambda r: pl.BlockSpec((None, r, tm), lambda bi, j: (bi, 0, j))
    trc = lambda r: pl.BlockSpec((None, tm // CH, r, CH), lambda bi, j: (bi, j, 0, 0))
    full = lambda a: pl.BlockSpec(a.shape, lambda bi, j: (0, 0))
    sd = jax.ShapeDtypeStruct
    out_shape = [sd((b, s, 128), bf), sd((b, s, 128), bf), sd((b, s, 128), bf), sd((b, s, 128), bf),
                 sd((b, s, 128), f32), sd((b, s, 128), f32),
                 sd((b, DSA_WIDTH, s), bf), sd((b, NSA_WIDTH, s), bf), sd((b, IDX_HEADS * IDX_DIM, s), bf),
                 sd((b, _G_ROWS, s), f32),
                 sd((b, s // CH, HEAD_DIM, CH), bf), sd((b, s // CH, KV_WIDTH, CH), bf), sd((b, s // CH, KV_WIDTH, CH), bf)]
    out_specs = [nat(128)] * 6 + [tr(DSA_WIDTH), tr(NSA_WIDTH), tr(IDX_HEADS * IDX_DIM), tr(_G_ROWS),
                                  trc(HEAD_DIM), trc(KV_WIDTH), trc(KV_WIDTH)]
    return pl.pallas_call(
        _in_proj_kernel,
        out_shape=out_shape,
        grid=(b, s // tm),
        in_specs=[nat(D_MODEL), nat(1), pl.BlockSpec((None, 1, tm), lambda bi, j: (bi, 0, j)),
                  full(fr), full(fc), full(wn), full(bn), full(wt), full(bt)],
        out_specs=out_specs,
        compiler_params=_cparams(2),
        name="in_proj",
    )(x, posc, posr, fr, fc, wn, bn, wt, bt)


def _compress_kernel(kc_ref, vc_ref, pos_ref, w1_ref, w2_ref, kcn_ref, vcT_ref):
    s = kc_ref.shape[0]
    n_chunk = s // CMP_STRIDE
    if n_chunk < LANES:
        kcn_ref[...] = jnp.zeros(kcn_ref.shape, kcn_ref.dtype)
        vcT_ref[...] = jnp.zeros(vcT_ref.shape, vcT_ref.dtype)
    for kv, src_ref in enumerate((kc_ref, vc_ref)):
        top = [jnp.zeros((n_chunk, CMP_HIDDEN), jnp.float32) for _ in range(NSA_KV_GROUPS)]
        bot = [jnp.zeros((n_chunk, CMP_HIDDEN), jnp.float32) for _ in range(NSA_KV_GROUPS)]
        for l in range(CMP_STRIDE):
            rows_all = src_ref[pl.ds(l, n_chunk, stride=CMP_STRIDE), :]
            w_top = w1_ref[kv, l * HEAD_DIM:(l + 1) * HEAD_DIM, :]
            w_bot = w1_ref[kv, (CMP_STRIDE + l) * HEAD_DIM:(CMP_STRIDE + l + 1) * HEAD_DIM, :]
            for g in range(NSA_KV_GROUPS):
                rows = rows_all[:, HEAD_DIM * g:HEAD_DIM * (g + 1)]
                a = (rows + pos_ref[kv, l:l + 1, :]).astype(MXU_DTYPE)
                bm = (rows + pos_ref[kv, CMP_STRIDE + l:CMP_STRIDE + l + 1, :]).astype(MXU_DTYPE)
                top[g] = top[g] + _dot(a, w_top)
                bot[g] = bot[g] + _dot(bm, w_bot)
        for g in range(NSA_KV_GROUPS):
            h = top[g] + pltpu.roll(bot[g], n_chunk - 1, 0)
            y = _dot(jax.nn.gelu(h).astype(MXU_DTYPE), w2_ref[kv])
            ridx = lax.broadcasted_iota(jnp.int32, y.shape, 0)
            y = jnp.where(ridx < n_chunk - 1, y, 0.0)
            if kv == 0:
                kcn_ref[0:n_chunk, HEAD_DIM * g:HEAD_DIM * (g + 1)] = y.astype(kcn_ref.dtype)
            else:
                vcT_ref[HEAD_DIM * g:HEAD_DIM * (g + 1), 0:n_chunk] = y.T.astype(vcT_ref.dtype)


def _compress(kc3, vc3, cmp_pos, w1, w2):
    b, s, _ = kc3.shape
    src = pl.BlockSpec((None, s, KV_WIDTH), lambda i: (i, 0, 0))
    out = pl.BlockSpec((None, LANES, KV_WIDTH), lambda i: (i, 0, 0))
    return pl.pallas_call(
        _compress_kernel,
        out_shape=[jax.ShapeDtypeStruct((b, LANES, KV_WIDTH), MXU_DTYPE)] * 2,
        grid=(b,),
        in_specs=[src, src,
                  pl.BlockSpec(cmp_pos.shape, lambda i: (0, 0, 0)),
                  pl.BlockSpec(w1.shape, lambda i: (0, 0, 0)),
                  pl.BlockSpec(w2.shape, lambda i: (0, 0, 0))],
        out_specs=[out, out],
        compiler_params=_cparams(1),
        name="nsa_compress",
    )(kc3, vc3, cmp_pos, w1, w2)


def _stack_heads(qT, h0, nh, place, n_place):
    q = jnp.concatenate([qT[HEAD_DIM * (h0 + r):HEAD_DIM * (h0 + r + 1), :] for r in range(nh)], axis=1)
    z = jnp.zeros_like(q)
    return jnp.concatenate([q if p == place else z for p in range(n_place)], axis=0)


def _flash_attention(q_pad, nh, k_ref, vT_ref, v_r0, c_lo, c_hi, mask_fn, scr):
    w = nh * TQ
    s_a, s_b, p_a, p_b, acc = scr
    last = c_hi - 1

    def scores_into(s_ref, c):
        c_mem = jnp.minimum(c, last)
        off = pl.multiple_of(c_mem * CH, CH)
        s = _dot(k_ref[pl.ds(off, CH), :], q_pad)
        bias = jnp.where(mask_fn(c_mem, c), 0.0, -jnp.inf)
        for h in range(nh):
            s_ref[:, TQ * h:TQ * (h + 1)] = s[:, TQ * h:TQ * (h + 1)] + bias

    ones_rows = jnp.ones((PACKED_ROWS, CH), MXU_DTYPE)

    def add_weighted_values(c, p_ref):
        vT = jnp.concatenate([vT_ref[jnp.clip(c, c_lo, last)][v_r0:v_r0 + HEAD_DIM, :], ones_rows], axis=0)
        acc[...] += _dot(vT, p_ref[...])

    def softmax_step(s_ref, p_ref, m):
        m_new, alpha = [], []
        for h in range(nh):
            hs = slice(TQ * h, TQ * (h + 1))
            mh = jnp.maximum(m[:, hs], jnp.max(_fold8(s_ref[:, hs], jnp.maximum), axis=0, keepdims=True))
            p_ref[:, hs] = jnp.exp2(s_ref[:, hs] - mh).astype(p_ref.dtype)
            m_new.append(mh)
            alpha.append(jnp.exp2(m[:, hs] - mh))
        acc[...] *= jnp.concatenate(alpha, axis=1)
        return jnp.concatenate(m_new, axis=1)

    def body(t, m):
        c0 = c_lo + 2 * t
        scores_into(s_b, c0 + 1)
        add_weighted_values(c0 - 1, p_b)
        m = softmax_step(s_a, p_a, m)
        scores_into(s_a, c0 + 2)
        add_weighted_values(c0, p_a)
        return softmax_step(s_b, p_b, m)

    acc[...] = jnp.zeros(acc.shape, acc.dtype)
    p_b[...] = jnp.zeros(p_b.shape, p_b.dtype)
    scores_into(s_a, c_lo)
    n_pairs = (c_hi - c_lo + 1) // 2
    lax.fori_loop(0, n_pairs, body, jnp.full((1, w), NEG_BIG, jnp.float32))
    add_weighted_values(c_lo + 2 * n_pairs - 1, p_b)
    return acc[0:HEAD_DIM, :] / jnp.maximum(acc[HEAD_DIM:HEAD_DIM + 1, :], 1e-30)


def _flash_scratch(nh):
    w = nh * TQ
    return [pltpu.VMEM((CH, w), jnp.float32), pltpu.VMEM((CH, w), jnp.float32),
            pltpu.VMEM((CH, w), MXU_DTYPE), pltpu.VMEM((CH, w), MXU_DTYPE),
            pltpu.VMEM((HEAD_DIM + PACKED_ROWS, w), jnp.float32)]


def _group_rms_store(heads, gain_ref, o_ref):
    o = jnp.concatenate(heads, axis=0)
    ms = jnp.mean(o * o, axis=0, keepdims=True)
    o_ref[...] = (o * lax.rsqrt(ms + LN_EPS) * gain_ref[...]).astype(o_ref.dtype)


def _sortable_key(v):
    bits = lax.bitcast_convert_type(v + 0.0, jnp.int32)
    return jnp.where(bits < 0, bits ^ 0x7FFFFFFF, bits)


def _dsa_kernel(topk, dqT_ref, iqT_ref, gT_ref, dkv_ref, ikn_ref, dvT_ref, tri_ref, gain_ref, o_ref,
                key_scr, hi_scr, lo_scr, *scr):
    i = pl.program_id(1)
    t0 = i * TQ
    nk = (t0 + TQ - 1) // CH + 1
    krow = lax.broadcasted_iota(jnp.int32, (CH, TQ), 0)
    qpos = t0 + lax.broadcasted_iota(jnp.int32, (CH, TQ), 1)

    iq_pad = _stack_heads_idx(iqT_ref[...])
    w_row = jnp.concatenate([gT_ref[h:h + 1, :] for h in range(IDX_HEADS)], axis=1)

    def index_body(c, carry):
        off = pl.multiple_of(c * CH, CH)
        r = jnp.maximum(_dot(ikn_ref[pl.ds(off, CH), :], iq_pad), 0.0) * w_row
        acc = r[:, 0:TQ]
        for h in range(1, IDX_HEADS):
            acc = acc + r[:, TQ * h:TQ * (h + 1)]
        idx = jnp.where(off + krow <= qpos, acc * IDX_SCALE, -jnp.inf)
        key = _sortable_key(idx)
        key_scr[c] = key
        hi_scr[c] = jnp.right_shift(key, 16).astype(jnp.int16)
        lo_scr[c] = ((key & 0xFFFF) + HALF_MIN).astype(jnp.int16)
        return carry

    lax.fori_loop(0, nk, index_body, 0)

    def count_ge(cand):
        cb = jnp.broadcast_to(cand, (CH, TQ))

        def body(c, acc):
            return acc + _fold8(jnp.where(key_scr[c] >= cb, 1.0, 0.0), jnp.add)

        part = lax.fori_loop(0, nk, body, jnp.zeros((SUBLANES, TQ), jnp.float32))
        return jnp.sum(part, axis=0, keepdims=True)

    def count16(half_scr, cand, strict=False):
        cb = jnp.broadcast_to(cand.astype(jnp.int16), (CH, TQ))

        def body(c, acc):
            h = half_scr[c]
            hit = (h > cb) if strict else (h >= cb)
            return acc + _fold_rows(jnp.where(hit, jnp.int16(1), jnp.int16(0)), PACKED_ROWS)

        part = lax.fori_loop(0, nk, body, jnp.zeros((PACKED_ROWS, TQ), jnp.int16))
        return jnp.sum(part.astype(jnp.float32), axis=0, keepdims=True)

    def kth_largest_half(half_scr, want, cnt_all):
        lo0 = jnp.full((1, TQ), HALF_MIN, jnp.int32)
        cand = jnp.zeros((1, TQ), jnp.int32)
        cnt = count16(half_scr, cand)
        ok = cnt >= want
        init = (jnp.where(ok, cand, lo0), jnp.where(ok, cnt, cnt_all))

        def bit_body(t, carry):
            lo, cnt_lo = carry
            cand = lo + jnp.left_shift(jnp.int32(1), 14 - t)
            cnt = count16(half_scr, cand)
            ok = cnt >= want
            return jnp.where(ok, cand, lo), jnp.where(ok, cnt, cnt_lo)

        return lax.fori_loop(0, 15, bit_body, init)

    kf = jnp.float32(topk)
    n_all = jnp.zeros((1, TQ), jnp.float32) + (nk * CH).astype(jnp.float32)
    tau_hi, cnt_ge_hi = kth_largest_half(hi_scr, kf, n_all)
    cnt_gt_hi = count16(hi_scr, tau_hi, strict=True)
    tau_hi_b = jnp.broadcast_to(tau_hi.astype(jnp.int16), (CH, TQ))

    def bucket_body(c, carry):
        lo_scr[c] = jnp.where(hi_scr[c] == tau_hi_b, lo_scr[c], jnp.int16(HALF_MIN))
        return carry

    lax.fori_loop(0, nk, bucket_body, 0)
    tau_lo, cnt_ge_lo = kth_largest_half(lo_scr, kf - cnt_gt_hi, cnt_ge_hi - cnt_gt_hi)
    tau = tau_hi * 65536 + (tau_lo - HALF_MIN)
    cnt_ge_tau = cnt_gt_hi + cnt_ge_lo

    tie_q = (cnt_ge_tau > kf) & (tau > _KEY_NEG_INF)

    @pl.when(jnp.max(jnp.where(tie_q, 1.0, 0.0)) > 0.0)
    def _():
        need = kf - count_ge(tau + 1)
        taub = jnp.broadcast_to(tau, (CH, TQ))
        live = jnp.broadcast_to(tie_q, (CH, TQ))

        def tie_body(c, seen):
            k = key_scr[c]
            tie = (k == taub) & live
            tf = jnp.where(tie, 1.0, 0.0)
            before = seen + _dot(tri_ref[...], tf.astype(MXU_DTYPE))
            key_scr[c] = jnp.where(tie & (before >= need), taub - 1, k)
            return seen + jnp.sum(_fold8(tf, jnp.add), axis=0, keepdims=True)

        lax.fori_loop(0, nk, tie_body, jnp.zeros((1, TQ), jnp.float32))

    taub = jnp.broadcast_to(tau, (CH, TQ))

    def mask_fn(c_mem, c_pos):
        return (key_scr[c_mem] >= taub) & (c_pos * CH + krow <= qpos)

    q_pad = _stack_heads(dqT_ref[...], 0, DSA_HEADS, 0, 2)
    oT = _flash_attention(q_pad, DSA_HEADS, dkv_ref, dvT_ref, 0, 0, nk, mask_fn, scr)
    _group_rms_store([oT[:, TQ * h:TQ * (h + 1)] for h in range(DSA_HEADS)], gain_ref, o_ref)


def _stack_heads_idx(iqT):
    q = jnp.concatenate([iqT[IDX_DIM * h:IDX_DIM * (h + 1), :] for h in range(IDX_HEADS)], axis=1)
    return jnp.concatenate([q, jnp.zeros((LANES - IDX_DIM, q.shape[1]), q.dtype)], axis=0)


def _dsa(dqT, iqT, gT, dkv, ikn, dvT, tri, gain):
    b, _, s = dqT.shape
    topk = min(IDX_TOPK, s // 4)
    qblk = lambda r: pl.BlockSpec((None, r, TQ), lambda bi, i: (bi, 0, i))
    sblk = lambda w: pl.BlockSpec((None, s, w), lambda bi, i: (bi, 0, 0))
    const = lambda a: pl.BlockSpec(a.shape, lambda bi, i: (0,) * a.ndim)
    return pl.pallas_call(
        functools.partial(_dsa_kernel, topk),
        out_shape=jax.ShapeDtypeStruct((b, DSA_WIDTH, s), MXU_DTYPE),
        grid=(b, s // TQ),
        in_specs=[qblk(DSA_WIDTH), qblk(IDX_HEADS * IDX_DIM), qblk(_G_ROWS), sblk(LANES), sblk(LANES),
                  pl.BlockSpec((None, s // CH, HEAD_DIM, CH), lambda bi, i: (bi, 0, 0, 0)),
                  const(tri), const(gain)],
        out_specs=qblk(DSA_WIDTH),
        scratch_shapes=[pltpu.VMEM((s // CH, CH, TQ), jnp.int32), pltpu.VMEM((s // CH, CH, TQ), jnp.int16),
                        pltpu.VMEM((s // CH, CH, TQ), jnp.int16)] + _flash_scratch(DSA_HEADS),
        compiler_params=_cparams(2),
        name="dsa_attention",
    )(dqT, iqT, gT, dkv, ikn, dvT, tri, gain)


def _split3(v):
    a = v.astype(MXU_DTYPE)
    r = v - a.astype(jnp.float32)
    b = r.astype(MXU_DTYPE)
    c = (r - b.astype(jnp.float32)).astype(MXU_DTYPE)
    return a, b, c


def _nsa_kernel(n_cmp, n_slc, n_sel, nqT_ref, gT_ref, kcn_ref, vcT_ref, ks_ref, vsT_ref, kw_ref, vwT_ref,
                ovT_ref, gain_ref, o_ref, sel_scr, *scr):
    i = pl.program_id(1)
    t0 = i * TQ
    krow = lax.broadcasted_iota(jnp.int32, (CH, TQ), 0)
    qpos = t0 + lax.broadcasted_iota(jnp.int32, (CH, TQ), 1)
    crow = lax.broadcasted_iota(jnp.int32, (LANES, TQ), 0)
    qpos_c = t0 + lax.broadcasted_iota(jnp.int32, (LANES, TQ), 1)
    jrow = lax.broadcasted_iota(jnp.int32, (n_slc, TQ), 0)
    qpos_j = t0 + lax.broadcasted_iota(jnp.int32, (n_slc, TQ), 1)
    nqT = nqT_ref[...]
    R = NSA_GROUP_SIZE
    n_buf = len(scr) // (2 * NSA_KV_GROUPS)
    scr = [scr[n_buf * k:n_buf * (k + 1)] for k in range(2 * NSA_KV_GROUPS)]
    heads_out = []
    for g in range(NSA_KV_GROUPS):
        q_pad = _stack_heads(nqT, R * g, R, g, NSA_KV_GROUPS)
        sc = _dot(kcn_ref[...], q_pad)
        cmask = (CMP_STRIDE * crow + CMP_BLOCK - 1 <= qpos_c) & (crow < n_cmp)
        p_blocks, psum = [], None
        for r in range(R):
            blk = jnp.where(cmask, sc[:, TQ * r:TQ * (r + 1)], -jnp.inf)
            m = jnp.max(blk, axis=0, keepdims=True)
            m = jnp.where(m > -jnp.inf, m, 0.0)
            p = jnp.exp2(blk - m)
            p = p / jnp.maximum(jnp.sum(p, axis=0, keepdims=True), 1e-30)
            p_blocks.append(p.astype(MXU_DTYPE))
            psum = p if psum is None else psum + p
        o_c = _dot(vcT_ref[HEAD_DIM * g:HEAD_DIM * (g + 1), :], jnp.concatenate(p_blocks, axis=1))
        pa, pb, pc = _split3(psum)
        ov = ovT_ref[0:n_slc, :]
        imp = _dot(ov, pa) + _dot(ov, pb) + _dot(ov, pc)
        cur = qpos_j // SLC_BLOCK
        forced = (jrow == 0) | (jrow == cur) | (jrow == cur - 1)
        visible = SLC_BLOCK * jrow <= qpos_j
        work = jnp.where(visible, jnp.where(forced, FORCE_SCORE, imp), -1.0)
        rank = jnp.zeros(work.shape, jnp.float32)
        for j in range(n_slc):
            wj = jnp.broadcast_to(work[j:j + 1, :], work.shape)
            ahead = (wj > work) | ((wj == work) & (jrow > j))
            rank = rank + jnp.where(ahead, 1.0, 0.0)
        sel_scr[0:n_slc, :] = jnp.where((rank < n_sel) & visible, 1.0, 0.0)

        def slc_mask(c_mem, c_pos):
            b0 = c_mem * (CH // SLC_BLOCK)
            rows = [jnp.broadcast_to(sel_scr[pl.ds(b0 + b, 1), :], (SLC_BLOCK, TQ)) for b in range(CH // SLC_BLOCK)]
            return (jnp.concatenate(rows, axis=0) > 0.5) & (c_pos * CH + krow <= qpos)

        o_s = _flash_attention(q_pad, R, ks_ref, vsT_ref, HEAD_DIM * g, 0, (t0 + TQ - 1) // CH + 1, slc_mask, scr[2 * g])

        def win_mask(c_mem, c_pos):
            d = qpos - (c_pos * CH + krow)
            return (d >= 0) & (d < WINDOW)

        o_w = _flash_attention(q_pad, R, kw_ref, vwT_ref, HEAD_DIM * g,
                               jnp.maximum(t0 - WINDOW + 1, 0) // CH, (t0 + TQ - 1) // CH + 1, win_mask, scr[2 * g + 1])
        for r in range(R):
            gr = IDX_HEADS + 3 * (R * g + r)
            hs = slice(TQ * r, TQ * (r + 1))
            heads_out.append(gT_ref[gr:gr + 1, :] * o_c[:, hs] + gT_ref[gr + 1:gr + 2, :] * o_s[:, hs]
                             + gT_ref[gr + 2:gr + 3, :] * o_w[:, hs])
    _group_rms_store(heads_out, gain_ref, o_ref)


def _nsa(nqT, gT, kcn, vcT, ks, vsT, kw, vwT, ovT, gain):
    b, _, s = nqT.shape
    n_slc = s // SLC_BLOCK
    n_sel = min(SLC_COUNT, n_slc)
    qblk = lambda r: pl.BlockSpec((None, r, TQ), lambda bi, i: (bi, 0, i))
    bfull = lambda a: pl.BlockSpec((None,) + a.shape[1:], lambda bi, i: (bi,) + (0,) * (a.ndim - 1))
    const = lambda a: pl.BlockSpec(a.shape, lambda bi, i: (0,) * a.ndim)
    return pl.pallas_call(
        functools.partial(_nsa_kernel, s // CMP_STRIDE - 1, n_slc, n_sel),
        out_shape=jax.ShapeDtypeStruct((b, NSA_WIDTH, s), MXU_DTYPE),
        grid=(b, s // TQ),
        in_specs=[qblk(NSA_WIDTH), qblk(_G_ROWS), bfull(kcn), bfull(vcT), bfull(ks), bfull(vsT), bfull(kw), bfull(vwT),
                  const(ovT), const(gain)],
        out_specs=qblk(NSA_WIDTH),
        scratch_shapes=[pltpu.VMEM((LANES, TQ), jnp.float32)] + 2 * NSA_KV_GROUPS * _flash_scratch(NSA_GROUP_SIZE),
        compiler_params=_cparams(2),
        name="nsa_attention",
    )(nqT, gT, kcn, vcT, ks, vsT, kw, vwT, ovT, gain)


def _out_proj_kernel(odT_ref, onT_ref, x_ref, w_ref, g_ref, b_ref, h_ref):
    y = _dot_tn(odT_ref[...], w_ref[0:DSA_WIDTH, :]) + _dot_tn(onT_ref[...], w_ref[DSA_WIDTH:, :])
    h_ref[...] = _layer_norm_rows(ALPHA * x_ref[...] + y, g_ref[...], b_ref[...])


def _out_proj(odT, onT, x, w, g, b_):
    b, s, _ = x.shape
    tm = TM_PROJ
    tr = lambda r: pl.BlockSpec((None, r, tm), lambda bi, j: (bi, 0, j))
    nat = pl.BlockSpec((None, tm, D_MODEL), lambda bi, j: (bi, j, 0))
    full = lambda a: pl.BlockSpec(a.shape, lambda bi, j: (0, 0))
    return pl.pallas_call(
        _out_proj_kernel,
        out_shape=jax.ShapeDtypeStruct((b, s, D_MODEL), jnp.float32),
        grid=(b, s // tm),
        in_specs=[tr(DSA_WIDTH), tr(NSA_WIDTH), nat, full(w), full(g), full(b_)],
        out_specs=nat,
        compiler_params=_cparams(2),
        name="out_proj_ln",
    )(odT, onT, x, w, g, b_)


def _ffn_kernel(h_ref, wg_ref, wu_ref, wd_ref, g_ref, b_ref, o_ref):
    h = h_ref[...]
    hb = h.astype(MXU_DTYPE)
    acc = jnp.zeros(h.shape, jnp.float32)
    for c in range(FFN_HIDDEN // FFN_CHUNK):
        sl = slice(FFN_CHUNK * c, FFN_CHUNK * (c + 1))
        gate = _dot(hb, wg_ref[:, sl])
        up = _dot(hb, wu_ref[:, sl])
        acc = acc + _dot((jax.nn.silu(gate) * up).astype(MXU_DTYPE), wd_ref[sl, :])
    o_ref[...] = _layer_norm_rows(ALPHA * h + acc, g_ref[...], b_ref[...])


def _ffn(h, wg, wu, wd, g, b):
    n = h.shape[0]
    tm = TM_PROJ
    row = pl.BlockSpec((tm, D_MODEL), lambda i: (i, 0))
    full = lambda a: pl.BlockSpec(a.shape, lambda i: (0, 0))
    return pl.pallas_call(
        _ffn_kernel,
        out_shape=jax.ShapeDtypeStruct((n, D_MODEL), jnp.float32),
        grid=(n // tm,),
        in_specs=[row, full(wg), full(wu), full(wd), full(g), full(b)],
        out_specs=row,
        compiler_params=_cparams(1),
        name="ffn_ln",
    )(h, wg, wu, wd, g, b)


def _rope_tables():
    def inv_freq(half):
        return jnp.power(ROPE_THETA, -jnp.arange(half, dtype=jnp.float32) / half)

    f8, g4 = inv_freq(HEAD_DIM // 8), inv_freq(IDX_DIM // 8)
    head = jnp.concatenate([f8, f8, jnp.zeros((HEAD_DIM - 16,), jnp.float32)])
    idx = jnp.concatenate([g4, g4, jnp.zeros((IDX_DIM - 8,), jnp.float32)])
    fr = jnp.concatenate([head, idx, idx])[None, :]
    sign = jnp.concatenate([jnp.zeros((8,)), -jnp.ones((4,)), jnp.ones((4,))]).astype(jnp.float32)
    fc = jnp.stack([jnp.concatenate([f8, g4, g4]), sign], axis=1)
    return fr, jnp.pad(fc, ((0, 0), (0, LANES - 2)))


def _split_in_proj(w, b):
    sl = lambda a, o, n: a[..., o:o + n]
    zpad = lambda a, n: jnp.zeros(a.shape[:-1] + (n,), a.dtype)

    def nat(a):
        return jnp.concatenate([sl(a, _O_DK, 128), sl(a, _O_KS, 128), sl(a, _O_KW, 128), sl(a, _O_IK, IDX_DIM),
                                zpad(a, LANES - IDX_DIM), sl(a, _O_KC, 128), sl(a, _O_VC, 128)], axis=-1)

    def tr(a):
        return jnp.concatenate([sl(a, _O_DQ, 512), sl(a, _O_NQ, 512), sl(a, _O_IQ, 256), sl(a, _O_DV, 64),
                                sl(a, _O_VS, 128), sl(a, _O_VW, 128), sl(a, _O_IW, IDX_HEADS),
                                sl(a, _O_NG, 3 * NSA_HEADS)], axis=-1)

    return nat(w).astype(MXU_DTYPE), nat(b)[None, :], tr(w).T.astype(MXU_DTYPE), tr(b)[:, None]


def _selection_tables(s):
    n_chunk = s // CMP_STRIDE
    n_slc = s // SLC_BLOCK
    j = np.arange(LANES)[:, None]
    c = np.arange(LANES)[None, :]
    ovT = ((c * CMP_STRIDE < j * SLC_BLOCK + SLC_BLOCK) & (c * CMP_STRIDE + CMP_BLOCK - 1 >= j * SLC_BLOCK)
           & (c < n_chunk - 1) & (j < n_slc))
    tri = np.arange(CH)[None, :] < np.arange(CH)[:, None]
    as_mxu = lambda a: jnp.asarray(a.astype(np.float32), MXU_DTYPE)
    return as_mxu(ovT), as_mxu(tri)


def _layer(x, positions, w_in, b_in, cmp_pos, cmp_w1, cmp_w2, mix_gain, w_out,
           ln1_g, ln1_b, w_gate, w_up, w_down, ln2_g, ln2_b):
    b, s, d = x.shape
    assert d == D_MODEL and s % TM_PROJ == 0 and TM_PROJ % CH == 0 and CH % TQ == 0 and TQ == LANES
    assert s // CMP_STRIDE <= LANES and s // SLC_BLOCK <= LANES
    n = b * s
    fr, fc = _rope_tables()
    wn, bn, wt, bt = _split_in_proj(w_in, b_in)
    (dkv, ks, kw, ikn, kcf, vcf, dqT, nqT, iqT, gT, dvT, vsT, vwT) = _in_proj(
        x, positions[:, :, None], positions[:, None, :], fr, fc, wn, bn, wt, bt)
    ovT, tri = _selection_tables(s)
    kcn, vcT = _compress(kcf, vcf, cmp_pos, cmp_w1.astype(MXU_DTYPE), cmp_w2.astype(MXU_DTYPE))
    gain = jnp.broadcast_to(mix_gain[:, None], (D_MODEL, LANES))
    o_dsa = _dsa(dqT, iqT, gT, dkv, ikn, dvT, tri, gain[:DSA_WIDTH])
    o_nsa = _nsa(nqT, gT, kcn, vcT, ks, vsT, kw, vwT, ovT, gain[DSA_WIDTH:])
    h = _out_proj(o_dsa, o_nsa, x, w_out.astype(MXU_DTYPE), ln1_g[None, :], ln1_b[None, :])
    y = _ffn(h.reshape(n, d), w_gate.astype(MXU_DTYPE), w_up.astype(MXU_DTYPE), w_down.astype(MXU_DTYPE),
             ln2_g[None, :], ln2_b[None, :])
    return y.reshape(b, s, d)


def kernel(x, positions, w_in, b_in, cmp_pos, cmp_w1, cmp_w2, mix_gain, w_out, ln1_g, ln1_b, w_gate, w_up, w_down, ln2_g, ln2_b):
    for l in range(DEPTH):
        x = _layer(x, positions, w_in[l], b_in[l], cmp_pos[l], cmp_w1[l], cmp_w2[l], mix_gain[l], w_out[l],
                   ln1_g[l], ln1_b[l], w_gate[l], w_up[l], w_down[l], ln2_g[l], ln2_b[l])
    return x
```
